```python
import math
import jax, jax.numpy as jnp
from jax import lax
import numpy as np

D_MODEL = 1024
BATCH = 8
SEQ = 4096
DEPTH = 1

NSA_HEADS = 8
NSA_KV_GROUPS = 2
HEAD_DIM = 64
Q_PER_KV = NSA_HEADS // NSA_KV_GROUPS
NSA_WIDTH = NSA_HEADS * HEAD_DIM
KV_WIDTH = NSA_KV_GROUPS * HEAD_DIM
CMP_BLOCK = 32
CMP_STRIDE = 16
CMP_HIDDEN = 256
SLC_BLOCK = 64
SLC_TOPK = 16
SLC_QBLOCK = 64
WINDOW = 512
WIN_BLOCK = 128
FORCE_BONUS = 1000.0
SSM_GROUP = 16
SSM_GROUPS = 32
SSM_STATE = 64
SSM_WIDTH = SSM_GROUPS * SSM_GROUP
D_FF = 2816
RMS_EPS = 1e-6
IN_WIDTH = NSA_WIDTH + 6 * KV_WIDTH + 3 * NSA_HEADS + SSM_WIDTH + 2 * D_MODEL

kernel_name = "nsa_s5_gated_macaron_layer"


def _rmsnorm(x, g):
    xf = x.astype(jnp.float32)
    return xf * lax.rsqrt(jnp.mean(xf * xf, axis=-1, keepdims=True) + RMS_EPS) * g.astype(jnp.float32)


def _swiglu(x, g, w_gate, w_up, w_down):
    h = _rmsnorm(x, g)
    return (jax.nn.silu(h @ w_gate) * (h @ w_up)) @ w_down


def _masked_softmax(s, mask):
    s = jnp.where(mask, s.astype(jnp.float32), -1e30)
    m = jnp.max(s, axis=-1, keepdims=True)
    e = jnp.where(mask, jnp.exp(s - m), 0.0)
    den = jnp.sum(e, axis=-1, keepdims=True)
    return e / jnp.where(den > 0, den, 1.0)


def _compress(kv, pos, w1, w2):
    b, s, g, d = kv.shape
    chunks = kv.reshape(b, s // CMP_STRIDE, CMP_STRIDE, g, d)
    blocks = jnp.concatenate([chunks[:, :-1], chunks[:, 1:]], axis=2) + pos[:, None, :]
    flat = blocks.transpose(0, 1, 3, 2, 4).reshape(b, s // CMP_STRIDE - 1, g, CMP_BLOCK * d)
    return jax.nn.gelu(flat @ w1) @ w2


def _selected_attention(q, k, v, sel, scale):
    b, s, g, r, d = q.shape
    n = sel.shape[-1]
    ns = s // SLC_BLOCK
    nq = s // SLC_QBLOCK
    kb = k.reshape(b, ns, SLC_BLOCK, g, d).transpose(0, 3, 1, 2, 4)
    vb = v.reshape(b, ns, SLC_BLOCK, g, d).transpose(0, 3, 1, 2, 4)
    qc = q.reshape(b, nq, SLC_QBLOCK, g, r, d).transpose(1, 0, 2, 3, 4, 5)
    ic = sel.reshape(b, g, nq, SLC_QBLOCK, n).transpose(2, 0, 1, 3, 4)
    tc = jnp.arange(s).reshape(nq, SLC_QBLOCK)
    bi = jnp.arange(b)[:, None, None, None]
    gi = jnp.arange(g)[None, :, None, None]
    offs = jnp.arange(SLC_BLOCK)

    def step(args):
        qb, ib, tb = args
        kg = kb[bi, gi, ib]
        vg = vb[bi, gi, ib]
        sc = jnp.einsum('bqgrd,bgqnkd->bgrqnk', qb, kg) * scale
        kpos = ib[..., None] * SLC_BLOCK + offs
        mask = (kpos <= tb[:, None, None])[:, :, None]
        p = _masked_softmax(sc.reshape(b, g, r, SLC_QBLOCK, n * SLC_BLOCK),
                            mask.reshape(b, g, 1, SLC_QBLOCK, n * SLC_BLOCK))
        return jnp.einsum('bgrqk,bgqkd->bqgrd', p,
                          vg.reshape(b, g, SLC_QBLOCK, n * SLC_BLOCK, d).astype(jnp.float32))

    o = lax.map(step, (qc, ic, tc))
    return o.transpose(1, 0, 2, 3, 4, 5).reshape(b, s, g, r, d)


def _window_attention(q, k, v, scale):
    b, s, g, r, d = q.shape
    nb = s // WIN_BLOCK
    nprev = WINDOW // WIN_BLOCK
    pad = ((0, 0), (WINDOW, 0), (0, 0), (0, 0))
    kp = jnp.pad(k, pad).reshape(b, nb + nprev, WIN_BLOCK, g, d)
    vp = jnp.pad(v, pad).reshape(b, nb + nprev, WIN_BLOCK, g, d)
    band_k = jnp.concatenate([kp[:, i:i + nb] for i in range(nprev + 1)], axis=2)
    band_v = jnp.concatenate([vp[:, i:i + nb] for i in range(nprev + 1)], axis=2)
    qw = q.reshape(b, nb, WIN_BLOCK, g, r, d)
    sc = jnp.einsum('bnqgrd,bnkgd->bgrnqk', qw, band_k) * scale
    blk = jnp.arange(nb)[:, None] * WIN_BLOCK
    qpos = blk + jnp.arange(WIN_BLOCK)[None, :]
    kpos = blk - WINDOW + jnp.arange((nprev + 1) * WIN_BLOCK)[None, :]
    diff = qpos[:, :, None] - kpos[:, None, :]
    mask = (diff >= 0) & (diff < WINDOW) & (kpos[:, None, :] >= 0)
    p = _masked_softmax(sc, mask)
    o = jnp.einsum('bgrnqk,bnkgd->bnqgrd', p, band_v.astype(jnp.float32))
    return o.reshape(b, s, g, r, d)


def _nsa(q, k_cmp, v_cmp, k_slc, v_slc, k_win, v_win, gate_logits, q_norm,
         k_norm_cmp, k_norm_slc, k_norm_win, cmp_pos_k, cmp_pos_v,
         cmp_k_w1, cmp_k_w2, cmp_v_w1, cmp_v_w2):
    b, s, g, r, d = q.shape
    scale = HEAD_DIM ** -0.5
    t = jnp.arange(s)
    q = _rmsnorm(q, q_norm)
    kc = _rmsnorm(_compress(k_cmp, cmp_pos_k, cmp_k_w1, cmp_k_w2), k_norm_cmp)
    vc = _compress(v_cmp, cmp_pos_v, cmp_v_w1, cmp_v_w2).astype(jnp.float32)
    nc = kc.shape[1]
    ci = jnp.arange(nc)
    sc = jnp.einsum('bsgrd,bcgd->bgrsc', q, kc) * scale
    cmp_mask = (ci * CMP_STRIDE + CMP_BLOCK - 1)[None, :] <= t[:, None]
    p_cmp = _masked_softmax(sc, cmp_mask)
    o_cmp = jnp.einsum('bgrsc,bcgd->bsgrd', p_cmp, vc)
    ns = s // SLC_BLOCK
    j = jnp.arange(ns)
    overlap = ((ci[:, None] * CMP_STRIDE < (j[None, :] + 1) * SLC_BLOCK) &
               (ci[:, None] * CMP_STRIDE + CMP_BLOCK > j[None, :] * SLC_BLOCK)).astype(jnp.float32)
    imp = jnp.einsum('bgrsc,cj->bgsj', p_cmp, overlap)
    qblk = t // SLC_BLOCK
    force = (j[None, :] == 0) | (j[None, :] == qblk[:, None]) | (j[None, :] == qblk[:, None] - 1)
    score = jnp.where(j[None, :] <= qblk[:, None], imp + FORCE_BONUS * force.astype(jnp.float32), -1e30)
    _, sel = lax.top_k(score, min(SLC_TOPK, ns))
    o_slc = _selected_attention(q, _rmsnorm(k_slc, k_norm_slc), v_slc, sel, scale)
    o_win = _window_attention(q, _rmsnorm(k_win, k_norm_win), v_win, scale)
    gates = jax.nn.sigmoid(gate_logits.astype(jnp.float32)).reshape(b, s, 3, g, r, 1)
    o = gates[:, :, 0] * o_cmp + gates[:, :, 1] * o_slc + gates[:, :, 2] * o_win
    return o.reshape(b, s, NSA_WIDTH)


def _complex_affine_combine(e1, e2):
    a1r, a1i, b1r, b1i = e1
    a2r, a2i, b2r, b2i = e2
    return (a2r * a1r - a2i * a1i,
            a2r * a1i + a2i * a1r,
            a2r * b1r - a2i * b1i + b2r,
            a2r * b1i + a2i * b1r + b2i)


def _s5_glu(u, lambda_re, lambda_im, log_step, b_re, b_im, c_re, c_im, d_skip, glu_w):
    b, s, _ = u.shape
    u = u.astype(jnp.float32).reshape(b, s, SSM_GROUPS, SSM_GROUP)
    lr = lambda_re.astype(jnp.float32)
    li = lambda_im.astype(jnp.float32)
    step = jnp.exp(log_step.astype(jnp.float32))[:, None]
    mag = jnp.exp(lr * step)
    ar = mag * jnp.cos(li * step)
    ai = mag * jnp.sin(li * step)
    den = lr * lr + li * li
    cr = ((ar - 1.0) * lr + ai * li) / den
    cim = (ai * lr - (ar - 1.0) * li) / den
    br = b_re.astype(jnp.float32)
    bim = b_im.astype(jnp.float32)
    bbr = cr[..., None] * br - cim[..., None] * bim
    bbi = cr[..., None] * bim + cim[..., None] * br
    ur = jnp.einsum('bsgc,gpc->bsgp', u, bbr)
    ui = jnp.einsum('bsgc,gpc->bsgp', u, bbi)
    a_r = jnp.broadcast_to(ar[None, None], (1, s, SSM_GROUPS, SSM_STATE))
    a_i = jnp.broadcast_to(ai[None, None], (1, s, SSM_GROUPS, SSM_STATE))
    _, _, xr, xi = lax.associative_scan(_complex_affine_combine, (a_r, a_i, ur, ui), axis=1)
    y = (jnp.einsum('bsgp,gcp->bsgc', xr, c_re.astype(jnp.float32))
         - jnp.einsum('bsgp,gcp->bsgc', xi, c_im.astype(jnp.float32))
         + d_skip.astype(jnp.float32) * u)
    hg = jax.nn.gelu(y.reshape(b, s, SSM_WIDTH)) @ glu_w
    val, gate = jnp.split(hg, 2, axis=-1)
    return val * jax.nn.sigmoid(gate)


def _mixing(x, mix_norm, w_in, q_norm, k_norm_cmp, k_norm_slc, k_norm_win, cmp_pos_k, cmp_pos_v,
            cmp_k_w1, cmp_k_w2, cmp_v_w1, cmp_v_w2, w_nsa_proj, ssm_lambda_re, ssm_lambda_im,
            ssm_log_step, ssm_b_re, ssm_b_im, ssm_c_re, ssm_c_im, ssm_d, ssm_glu_w, w_out):
    b, s, _ = x.shape
    h = _rmsnorm(x, mix_norm)
    proj = h @ w_in
    sizes = [NSA_WIDTH] + [KV_WIDTH] * 6 + [3 * NSA_HEADS, SSM_WIDTH, D_MODEL, D_MODEL]
    cuts = [sum(sizes[:i + 1]) for i in range(len(sizes) - 1)]
    q, kc, vc, ks, vs, kw, vw, nsa_gate, ssm_u, gate_nsa, gate_ssm = jnp.split(proj, cuts, axis=-1)
    kvr = lambda z: z.reshape(b, s, NSA_KV_GROUPS, HEAD_DIM)
    o_nsa = _nsa(q.reshape(b, s, NSA_KV_GROUPS, Q_PER_KV, HEAD_DIM), kvr(kc), kvr(vc), kvr(ks), kvr(vs),
                 kvr(kw), kvr(vw), nsa_gate, q_norm, k_norm_cmp, k_norm_slc, k_norm_win,
                 cmp_pos_k, cmp_pos_v, cmp_k_w1, cmp_k_w2, cmp_v_w1, cmp_v_w2)
    y_nsa = o_nsa @ w_nsa_proj
    y_ssm = _s5_glu(ssm_u, ssm_lambda_re, ssm_lambda_im, ssm_log_step, ssm_b_re, ssm_b_im,
                    ssm_c_re, ssm_c_im, ssm_d, ssm_glu_w)
    merged = jax.nn.sigmoid(gate_nsa) * y_nsa + jax.nn.sigmoid(gate_ssm) * y_ssm
    return merged @ w_out


def setup_inputs(seed: int = 0) -> dict:
    key = jax.random.key(seed)
    ks = jax.random.split(key, 32)
    L = DEPTH
    nrm = lambda k, shape, sc: jax.random.normal(k, shape, jnp.float32) * sc
    gain = lambda k, n: 1.0 + 0.01 * jax.random.normal(k, (L, n), jnp.float32)
    lam_n = jnp.arange(SSM_STATE, dtype=jnp.float32)
    return {
        'x': nrm(ks[0], (BATCH, SEQ, D_MODEL), 1.0),
        'ffn1_norm': gain(ks[1], D_MODEL),
        'ffn1_w_gate': nrm(ks[2], (L, D_MODEL, D_FF), D_MODEL ** -0.5),
        'ffn1_w_up': nrm(ks[3], (L, D_MODEL, D_FF), D_MODEL ** -0.5),
        'ffn1_w_down': nrm(ks[4], (L, D_FF, D_MODEL), D_FF ** -0.5),
        'mix_norm': gain(ks[5], D_MODEL),
        'w_in': nrm(ks[6], (L, D_MODEL, IN_WIDTH), D_MODEL ** -0.5),
        'q_norm': gain(ks[7], HEAD_DIM),
        'k_norm_cmp': gain(ks[8], HEAD_DIM),
        'k_norm_slc': gain(ks[9], HEAD_DIM),
        'k_norm_win': gain(ks[10], HEAD_DIM),
        'cmp_pos_k': nrm(ks[11], (L, CMP_BLOCK, HEAD_DIM), 0.1),
        'cmp_pos_v': nrm(ks[12], (L, CMP_BLOCK, HEAD_DIM), 0.1),
        'cmp_k_w1': nrm(ks[13], (L, CMP_BLOCK * HEAD_DIM, CMP_HIDDEN), (CMP_BLOCK * HEAD_DIM) ** -0.5),
        'cmp_k_w2': nrm(ks[14], (L, CMP_HIDDEN, HEAD_DIM), CMP_HIDDEN ** -0.5),
        'cmp_v_w1': nrm(ks[15], (L, CMP_BLOCK * HEAD_DIM, CMP_HIDDEN), (CMP_BLOCK * HEAD_DIM) ** -0.5),
        'cmp_v_w2': nrm(ks[16], (L, CMP_HIDDEN, HEAD_DIM), CMP_HIDDEN ** -0.5),
        'w_nsa_proj': nrm(ks[17], (L, NSA_WIDTH, D_MODEL), NSA_WIDTH ** -0.5),
        'ssm_lambda_re': -0.5 + nrm(ks[18], (L, SSM_GROUPS, SSM_STATE), 0.01),
        'ssm_lambda_im': math.pi * lam_n + nrm(ks[19], (L, SSM_GROUPS, SSM_STATE), 0.01),
        'ssm_log_step': jax.random.uniform(ks[20], (L, SSM_GROUPS), jnp.float32,
                                           minval=math.log(1e-3), maxval=math.log(1e-1)),
        'ssm_b_re': nrm(ks[21], (L, SSM_GROUPS, SSM_STATE, SSM_GROUP), (2 * SSM_GROUP) ** -0.5),
        'ssm_b_im': nrm(ks[22], (L, SSM_GROUPS, SSM_STATE, SSM_GROUP), (2 * SSM_GROUP) ** -0.5),
        'ssm_c_re': nrm(ks[23], (L, SSM_GROUPS, SSM_GROUP, SSM_STATE), (2 * SSM_STATE) ** -0.5),
        'ssm_c_im': nrm(ks[24], (L, SSM_GROUPS, SSM_GROUP, SSM_STATE), (2 * SSM_STATE) ** -0.5),
        'ssm_d': nrm(ks[25], (L, SSM_GROUPS, SSM_GROUP), 1.0),
        'ssm_glu_w': nrm(ks[26], (L, SSM_WIDTH, 2 * D_MODEL), SSM_WIDTH ** -0.5),
        'w_out': nrm(ks[27], (L, D_MODEL, D_MODEL), D_MODEL ** -0.5),
        'ffn2_norm': gain(ks[28], D_MODEL),
        'ffn2_w_gate': nrm(ks[29], (L, D_MODEL, D_FF), D_MODEL ** -0.5),
        'ffn2_w_up': nrm(ks[30], (L, D_MODEL, D_FF), D_MODEL ** -0.5),
        'ffn2_w_down': nrm(ks[31], (L, D_FF, D_MODEL), D_FF ** -0.5),
    }


def reference(x, ffn1_norm, ffn1_w_gate, ffn1_w_up, ffn1_w_down, mix_norm, w_in, q_norm,
              k_norm_cmp, k_norm_slc, k_norm_win, cmp_pos_k, cmp_pos_v, cmp_k_w1, cmp_k_w2,
              cmp_v_w1, cmp_v_w2, w_nsa_proj, ssm_lambda_re, ssm_lambda_im, ssm_log_step,
              ssm_b_re, ssm_b_im, ssm_c_re, ssm_c_im, ssm_d, ssm_glu_w, w_out,
              ffn2_norm, ffn2_w_gate, ffn2_w_up, ffn2_w_down):
    in_dtype = x.dtype
    h = x.astype(jnp.float32)
    for l in range(DEPTH):
        h = h + 0.5 * _swiglu(h, ffn1_norm[l], ffn1_w_gate[l], ffn1_w_up[l], ffn1_w_down[l])
        h = h + _mixing(h, mix_norm[l], w_in[l], q_norm[l], k_norm_cmp[l], k_norm_slc[l],
                        k_norm_win[l], cmp_pos_k[l], cmp_pos_v[l], cmp_k_w1[l], cmp_k_w2[l],
                        cmp_v_w1[l], cmp_v_w2[l], w_nsa_proj[l], ssm_lambda_re[l],
                        ssm_lambda_im[l], ssm_log_step[l], ssm_b_re[l], ssm_b_im[l],
                        ssm_c_re[l], ssm_c_im[l], ssm_d[l], ssm_glu_w[l], w_out[l])
        h = h + 0.5 * _swiglu(h, ffn2_norm[l], ffn2_w_gate[l], ffn2_w_up[l], ffn2_w_down[l])
    return h.astype(in_dtype)
```

```python
import functools
import math

import jax
import jax.numpy as jnp
from jax import lax
from jax.experimental import pallas as pl
from jax.experimental.pallas import tpu as pltpu

F32 = jnp.float32
BF16 = jnp.bfloat16

D_MODEL = 1024
NSA_HEADS = 8
NSA_KV_GROUPS = 2
HEAD_DIM = 64
Q_PER_KV = NSA_HEADS // NSA_KV_GROUPS
NSA_WIDTH = NSA_HEADS * HEAD_DIM
KV_WIDTH = NSA_KV_GROUPS * HEAD_DIM
CMP_BLOCK = 32
CMP_STRIDE = 16
CMP_HIDDEN = 256
SLC_BLOCK = 64
SLC_TOPK = 16
WINDOW = 512
FORCE_BONUS = 1000.0
SSM_GROUP = 16
SSM_GROUPS = 32
SSM_STATE = 64
SSM_WIDTH = SSM_GROUPS * SSM_GROUP
SSM_STATES = SSM_GROUPS * SSM_STATE
D_FF = 2816
RMS_EPS = 1e-6
NEG = -1e30

LANES = 128
SUBLANES = 8
VMEM_LIMIT = 56 * 1024 * 1024

P_Q = 0
P_KVC = P_Q + NSA_HEADS * LANES
P_KS = P_KVC + 2 * KV_WIDTH
P_VS = P_KS + NSA_KV_GROUPS * LANES
P_KW = P_VS + NSA_KV_GROUPS * LANES
P_VW = P_KW + NSA_KV_GROUPS * LANES
P_GATE = P_VW + NSA_KV_GROUPS * LANES
P_U = P_GATE + NSA_KV_GROUPS * LANES
P_END = P_U + SSM_WIDTH


def _params(sem, vmem=VMEM_LIMIT):
    return pltpu.CompilerParams(dimension_semantics=sem, vmem_limit_bytes=vmem)


def _const_spec(shape):
    n = len(shape)
    return pl.BlockSpec(shape, lambda *_: (0,) * n, pipeline_mode=pl.Buffered(1))


def _rms(x, g):
    return x * lax.rsqrt(jnp.mean(x * x, axis=-1, keepdims=True) + RMS_EPS) * g


def _slot_rms(x, g):
    ms = jnp.sum(x * x, axis=-1, keepdims=True) * (1.0 / HEAD_DIM)
    return x * lax.rsqrt(ms + RMS_EPS) * g


def _dot(a, b):
    return jnp.dot(a, b, preferred_element_type=F32)


def _dot_nt(a, b):
    return lax.dot_general(a, b, (((1,), (1,)), ((), ())), preferred_element_type=F32)


FF_CHUNK = D_FF // 2


def _ffn_body(x_ref, g_ref, wg_ref, wu_ref, wd_ref, o_ref):
    x = x_ref[...]
    hb = _rms(x, g_ref[...]).astype(BF16)
    acc = jnp.zeros_like(x)
    for c in range(D_FF // FF_CHUNK):
        sl = slice(c * FF_CHUNK, (c + 1) * FF_CHUNK)
        gate = _dot(hb, wg_ref[:, sl])
        up = _dot(hb, wu_ref[:, sl])
        act = (gate * jax.nn.sigmoid(gate) * up).astype(BF16)
        acc = acc + _dot(act, wd_ref[sl, :])
    o_ref[...] = x + 0.5 * acc


def _ffn(x2, g, wg, wu, wd, tm=512):
    t = x2.shape[0]
    return pl.pallas_call(
        _ffn_body,
        grid=(t // tm,),
        in_specs=[pl.BlockSpec((tm, D_MODEL), lambda i: (i, 0)),
                  _const_spec((1, D_MODEL)),
                  _const_spec((D_MODEL, D_FF)),
                  _const_spec((D_MODEL, D_FF)),
                  _const_spec((D_FF, D_MODEL))],
        out_specs=pl.BlockSpec((tm, D_MODEL), lambda i: (i, 0)),
        out_shape=jax.ShapeDtypeStruct((t, D_MODEL), F32),
        compiler_params=_params(("parallel",)),
        name="ffn",
    )(x2, g.reshape(1, D_MODEL), wg.astype(BF16), wu.astype(BF16), wd.astype(BF16))


def _proj_body(h_ref, g_ref, w_ref, qn_ref, ksn_ref, kwn_ref,
               q_ref, kvc_ref, ks_ref, vs_ref, kw_ref, vw_ref, gate_ref, u_ref):
    ts = h_ref.shape[1]
    hb = _rms(h_ref[0], g_ref[...]).astype(BF16)
    p = _dot(hb, w_ref[...])
    lane = lax.broadcasted_iota(jnp.int32, (ts, LANES), 1)
    tok = lax.broadcasted_iota(jnp.int32, (ts, LANES), 0) + pl.program_id(1) * ts
    low = lane < HEAD_DIM
    onehot = jnp.where(lane - HEAD_DIM == tok // SLC_BLOCK, 1.0, 0.0)
    qscale = HEAD_DIM ** -0.5
    for h in range(NSA_HEADS):
        qh = p[:, P_Q + h * LANES:P_Q + (h + 1) * LANES]
        q_ref[0, h] = (_slot_rms(qh, qn_ref[...]) * qscale).astype(BF16)
    kvc_ref[0] = p[:, P_KVC:P_KS]
    for g in range(NSA_KV_GROUPS):
        sl = lambda base: slice(base + g * LANES, base + (g + 1) * LANES)
        ks = _slot_rms(p[:, sl(P_KS)], ksn_ref[...])
        ks_ref[0, g] = jnp.where(low, ks, onehot).astype(BF16)
        vs_ref[0, g] = jnp.where(low, p[:, sl(P_VS)], 1.0).astype(BF16)
        kw_ref[0, g] = _slot_rms(p[:, sl(P_KW)], kwn_ref[...]).astype(BF16)
        vw_ref[0, g] = jnp.where(low, p[:, sl(P_VW)], 1.0).astype(BF16)
        gate_ref[0, g] = jax.nn.sigmoid(p[:, sl(P_GATE)])
    u_ref[0] = p[:, P_U:P_END]


def _slot_pad(w, n):
    d = w.shape[0]
    return jnp.pad(w.reshape(d, n, HEAD_DIM), ((0, 0), (0, 0), (0, LANES - HEAD_DIM))).reshape(d, n * LANES)


def _slot_gain(g):
    return jnp.pad(g, (0, LANES - HEAD_DIM)).reshape(1, LANES)


def _proj_weight(w_in):
    cuts = [NSA_WIDTH] + [KV_WIDTH] * 6 + [3 * NSA_HEADS, SSM_WIDTH]
    offs = [0]
    for c in cuts:
        offs.append(offs[-1] + c)
    seg = lambda i: w_in[:, offs[i]:offs[i + 1]]
    gate = seg(7).reshape(D_MODEL, 3, NSA_KV_GROUPS, Q_PER_KV).transpose(0, 2, 1, 3)
    gate = gate.reshape(D_MODEL, NSA_KV_GROUPS, 3 * Q_PER_KV)
    gate = jnp.pad(gate, ((0, 0), (0, 0), (0, LANES - 3 * Q_PER_KV))).reshape(D_MODEL, NSA_KV_GROUPS * LANES)
    cols = [_slot_pad(seg(0), NSA_HEADS), seg(1), seg(2)]
    cols += [_slot_pad(seg(i), NSA_KV_GROUPS) for i in (3, 4, 5, 6)]
    cols += [gate, seg(8)]
    return jnp.concatenate(cols, axis=1).astype(BF16)


def _proj(h, mix_norm, w_in, q_norm, k_norm_slc, k_norm_win, ts=512):
    b, s, _ = h.shape
    hm = lambda n: jax.ShapeDtypeStruct((b, n, s, LANES), BF16)
    hm_spec = lambda n: pl.BlockSpec((1, n, ts, LANES), lambda bi, i: (bi, 0, i, 0))
    row_spec = lambda w: pl.BlockSpec((1, ts, w), lambda bi, i: (bi, i, 0))
    return pl.pallas_call(
        _proj_body,
        grid=(b, s // ts),
        in_specs=[row_spec(D_MODEL), _const_spec((1, D_MODEL)), _const_spec((D_MODEL, P_END)),
                  _const_spec((1, LANES)), _const_spec((1, LANES)), _const_spec((1, LANES))],
        out_specs=[hm_spec(NSA_HEADS), row_spec(2 * KV_WIDTH),
                   hm_spec(NSA_KV_GROUPS), hm_spec(NSA_KV_GROUPS), hm_spec(NSA_KV_GROUPS), hm_spec(NSA_KV_GROUPS),
                   hm_spec(NSA_KV_GROUPS), row_spec(SSM_WIDTH)],
        out_shape=[hm(NSA_HEADS), jax.ShapeDtypeStruct((b, s, 2 * KV_WIDTH), F32),
                   hm(NSA_KV_GROUPS), hm(NSA_KV_GROUPS), hm(NSA_KV_GROUPS), hm(NSA_KV_GROUPS),
                   jax.ShapeDtypeStruct((b, NSA_KV_GROUPS, s, LANES), F32),
                   jax.ShapeDtypeStruct((b, s, SSM_WIDTH), F32)],
        compiler_params=_params(("parallel", "parallel")),
        name="proj",
    )(h, mix_norm.reshape(1, D_MODEL), _proj_weight(w_in),
      _slot_gain(q_norm), _slot_gain(k_norm_slc), _slot_gain(k_norm_win))


CMP_ROW = CMP_STRIDE * 2 * KV_WIDTH
CMP_HID_ALL = 4 * CMP_HIDDEN


def _compress_body(c_ref, wlo_ref, whi_ref, pk_ref, pv_ref, w1k_ref, w1v_ref, w2_ref, kn_ref,
                   kc_ref, vc_ref):
    nch = c_ref.shape[1]
    c = c_ref[0].astype(BF16)
    first = _dot(c, wlo_ref[...])
    second = _dot(c, whi_ref[...])
    second = pltpu.roll(second, nch - 1, 0)
    rows8 = lambda r: jnp.broadcast_to(r[...].astype(BF16), (SUBLANES, r.shape[1]))
    bk = _dot(rows8(pk_ref), w1k_ref[...])[0:1]
    bv = _dot(rows8(pv_ref), w1v_ref[...])[0:1]
    bias = jnp.concatenate([bk, bk, bv, bv], axis=1)
    hid = jax.nn.gelu(first + second + bias).astype(BF16)
    out = _dot(hid, w2_ref[...])
    lane = lax.broadcasted_iota(jnp.int32, (nch, LANES), 1)
    low = lane < HEAD_DIM
    for g in range(NSA_KV_GROUPS):
        kc_ref[0, g] = _slot_rms(out[:, g * LANES:(g + 1) * LANES], kn_ref[...]).astype(BF16)
        v = out[:, (NSA_KV_GROUPS + g) * LANES:(NSA_KV_GROUPS + g + 1) * LANES]
        vc_ref[0, g] = jnp.where(low, v, 1.0).astype(BF16)


def _compress_weights(w1k, w1v, w2k, w2v):
    def half(w1, lo):
        w = w1[lo * CMP_STRIDE * HEAD_DIM:(lo + 1) * CMP_STRIDE * HEAD_DIM].reshape(CMP_STRIDE, HEAD_DIM, CMP_HIDDEN)
        return w
    eye4 = jnp.eye(4, dtype=F32)
    def build(lo):
        wk, wv = half(w1k, lo), half(w1v, lo)
        w = jnp.stack([wk, wk, wv, wv], axis=1)
        w = w[:, :, :, None, :] * eye4[None, :, None, :, None]
        return w.reshape(CMP_ROW, CMP_HID_ALL).astype(BF16)
    w2 = jnp.stack([w2k, w2k, w2v, w2v], axis=0)
    w2 = jnp.pad(w2, ((0, 0), (0, 0), (0, LANES - HEAD_DIM)))
    w2 = w2[:, :, None, :] * eye4[:, None, :, None]
    return build(0), build(1), w2.reshape(CMP_HID_ALL, 4 * LANES).astype(BF16)


def _compress(kvc, cmp_pos_k, cmp_pos_v, w1k, w1v, w2k, w2v, k_norm_cmp):
    b, s, _ = kvc.shape
    nch = s // CMP_STRIDE
    wlo, whi, w2 = _compress_weights(w1k, w1v, w2k, w2v)
    flat = CMP_BLOCK * HEAD_DIM
    out = jax.ShapeDtypeStruct((b, NSA_KV_GROUPS, nch, LANES), BF16)
    out_spec = pl.BlockSpec((1, NSA_KV_GROUPS, nch, LANES), lambda bi: (bi, 0, 0, 0))
    return pl.pallas_call(
        _compress_body,
        grid=(b,),
        in_specs=[pl.BlockSpec((1, nch, CMP_ROW), lambda bi: (bi, 0, 0)),
                  _const_spec((CMP_ROW, CMP_HID_ALL)), _const_spec((CMP_ROW, CMP_HID_ALL)),
                  _const_spec((1, flat)), _const_spec((1, flat)),
                  _const_spec((flat, CMP_HIDDEN)), _const_spec((flat, CMP_HIDDEN)),
                  _const_spec((CMP_HID_ALL, 4 * LANES)), _const_spec((1, LANES))],
        out_specs=[out_spec, out_spec],
        out_shape=[out, out],
        compiler_params=_params(("parallel",)),
        name="compress",
    )(kvc.reshape(b, nch, CMP_ROW), wlo, whi, cmp_pos_k.reshape(1, flat), cmp_pos_v.reshape(1, flat),
      w1k.astype(BF16), w1v.astype(BF16), w2, _slot_gain(k_norm_cmp))


def _masked_softmax(s, mask, axis):
    s = jnp.where(mask, s, NEG)
    m = jnp.max(s, axis=axis, keepdims=True)
    e = jnp.where(mask, jnp.exp(s - m), 0.0)
    den = jnp.sum(e, axis=axis, keepdims=True)
    return e / jnp.where(den > 0, den, 1.0)


def _cmp_select_body(q_ref, kc_ref, vc_ref, gate_ref, o_ref, qs_ref, sc_ref, bias_ref):
    tq = q_ref.shape[2]
    nch = kc_ref.shape[2]
    ns = sc_ref.shape[0]
    t0 = pl.program_id(2) * tq
    q = q_ref[0].reshape(Q_PER_KV * tq, LANES)
    kc = kc_ref[0, 0]
    s = _dot_nt(q, kc).reshape(Q_PER_KV, tq, nch)
    tok = t0 + lax.broadcasted_iota(jnp.int32, (tq, nch), 0)
    cid = lax.broadcasted_iota(jnp.int32, (tq, nch), 1)
    p = _masked_softmax(s, (cid * CMP_STRIDE + CMP_BLOCK - 1 <= tok)[None], axis=-1)
    o = _dot(p.reshape(Q_PER_KV * tq, nch).astype(BF16), vc_ref[0, 0]).reshape(Q_PER_KV, tq, LANES)
    gate = gate_ref[0, 0]
    for r in range(Q_PER_KV):
        o_ref[0, r] = (o[r] * gate[:, r:r + 1]).astype(BF16)
    st = _dot_nt(kc, q)
    tok_t = t0 + lax.broadcasted_iota(jnp.int32, (nch, tq), 1)
    cid_t = lax.broadcasted_iota(jnp.int32, (nch, tq), 0)
    mask_t = cid_t * CMP_STRIDE + CMP_BLOCK - 1 <= tok_t
    psum = jnp.zeros((nch, tq), F32)
    for r in range(Q_PER_KV):
        psum = psum + _masked_softmax(st[:, r * tq:(r + 1) * tq], mask_t, axis=0)
    jb = lax.broadcasted_iota(jnp.int32, (ns, nch), 0)
    cb = lax.broadcasted_iota(jnp.int32, (ns, nch), 1)
    overlap = jnp.where((cb * CMP_STRIDE < (jb + 1) * SLC_BLOCK) & (cb * CMP_STRIDE + CMP_BLOCK > jb * SLC_BLOCK), 1.0, 0.0)
    imp = _dot(overlap, psum)
    j = lax.broadcasted_iota(jnp.int32, (ns, tq), 0)
    qblk = (t0 + lax.broadcasted_iota(jnp.int32, (ns, tq), 1)) // SLC_BLOCK
    force = (j == 0) | (j == qblk) | (j == qblk - 1)
    score = jnp.where(j <= qblk, imp + FORCE_BONUS * jnp.where(force, 1.0, 0.0), NEG)
    sc_ref[...] = score
    bias_ref[0:HEAD_DIM, :] = jnp.zeros((HEAD_DIM, tq), F32)
    if ns < LANES - HEAD_DIM:
        bias_ref[HEAD_DIM + ns:, :] = jnp.full((LANES - HEAD_DIM - ns, tq), NEG, F32)

    def rank_row(jj, carry):
        row = sc_ref[pl.ds(jj, 1), :]
        ahead = jnp.where(j < jj, jnp.where(score >= row, 1.0, 0.0), jnp.where(score > row, 1.0, 0.0))
        rank = jnp.sum(ahead, axis=0, keepdims=True)
        bias_ref[pl.ds(HEAD_DIM + jj, 1), :] = jnp.where(rank < SLC_TOPK, 0.0, NEG)
        return carry

    lax.fori_loop(0, ns, rank_row, 0)
    bias = bias_ref[...].T
    for r in range(Q_PER_KV):
        qs_ref[0, r] = (q_ref[0, r].astype(F32) + bias).astype(BF16)


def _cmp_select(q, kc, vc, gates, tq=256):
    b, _, s, _ = q.shape
    nch = kc.shape[2]
    ns = s // SLC_BLOCK
    assert ns <= LANES - HEAD_DIM
    qspec = pl.BlockSpec((1, Q_PER_KV, tq, LANES), lambda bi, g, i: (bi, g, i, 0))
    cspec = pl.BlockSpec((1, 1, nch, LANES), lambda bi, g, i: (bi, g, 0, 0))
    out = jax.ShapeDtypeStruct((b, NSA_HEADS, s, LANES), BF16)
    return pl.pallas_call(
        _cmp_select_body,
        grid=(b, NSA_KV_GROUPS, s // tq),
        in_specs=[qspec, cspec, cspec, pl.BlockSpec((1, 1, tq, LANES), lambda bi, g, i: (bi, g, i, 0))],
        out_specs=[qspec, qspec],
        out_shape=[out, out],
        scratch_shapes=[pltpu.VMEM((ns, tq), F32), pltpu.VMEM((LANES, tq), F32)],
        compiler_params=_params(("parallel", "parallel", "parallel")),
        name="cmp_select",
    )(q, kc, vc, gates)


def _slc_body(q_ref, k_ref, v_ref, gate_ref, o_ref):
    tq = q_ref.shape[2]
    rows = Q_PER_KV * tq
    i = pl.program_id(2)
    q = q_ref[0].reshape(rows, LANES)

    def tile(j, m, acc, causal):
        k0 = pl.multiple_of(j * tq, tq)
        s = _dot_nt(q, k_ref[0, 0, pl.ds(k0, tq), :])
        if causal:
            keep = lax.broadcasted_iota(jnp.int32, (tq, tq), 1) <= lax.broadcasted_iota(jnp.int32, (tq, tq), 0)
            s = jnp.where(keep[None], s.reshape(Q_PER_KV, tq, tq), NEG).reshape(rows, tq)
        m_new = jnp.maximum(m, jnp.max(s, axis=-1, keepdims=True))
        p = jnp.exp(s - m_new).astype(BF16)
        acc = jnp.exp(m - m_new) * acc + _dot(p, v_ref[0, 0, pl.ds(k0, tq), :])
        return m_new, acc

    init = (jnp.full((rows, 1), NEG, F32), jnp.zeros((rows, LANES), F32))
    m, acc = lax.fori_loop(0, i, lambda j, c: tile(j, c[0], c[1], False), init)
    m, acc = tile(i, m, acc, True)
    o = (acc / acc[:, HEAD_DIM:HEAD_DIM + 1]).reshape(Q_PER_KV, tq, LANES)
    gate = gate_ref[0, 0]
    for r in range(Q_PER_KV):
        o_ref[0, r] = (o[r] * gate[:, Q_PER_KV + r:Q_PER_KV + r + 1]).astype(BF16)


def _slc_attn(qs, k, v, gates, tq=256):
    b, _, s, _ = qs.shape
    qspec = pl.BlockSpec((1, Q_PER_KV, tq, LANES), lambda bi, g, i: (bi, g, i, 0))
    kspec = pl.BlockSpec((1, 1, s, LANES), lambda bi, g, i: (bi, g, 0, 0))
    return pl.pallas_call(
        _slc_body,
        grid=(b, NSA_KV_GROUPS, s // tq),
        in_specs=[qspec, kspec, kspec, pl.BlockSpec((1, 1, tq, LANES), lambda bi, g, i: (bi, g, i, 0))],
        out_specs=qspec,
        out_shape=jax.ShapeDtypeStruct((b, NSA_HEADS, s, LANES), BF16),
        compiler_params=_params(("parallel", "parallel", "arbitrary")),
        name="slc_attn",
    )(qs, k, v, gates)


def _win_body(q_ref, k_ref, v_ref, gate_ref, o_ref):
    tq = q_ref.shape[2]
    rows = Q_PER_KV * tq
    nprev = WINDOW // tq
    i = pl.program_id(2)
    q = q_ref[0].reshape(rows, LANES)
    qpos = i * tq + lax.broadcasted_iota(jnp.int32, (tq, tq), 0)
    col = lax.broadcasted_iota(jnp.int32, (tq, tq), 1)
    scores, starts = [], []
    for d in range(nprev + 1):
        jt = i - nprev + d
        k0 = pl.multiple_of(jnp.maximum(jt, 0) * tq, tq)
        diff = qpos - (jt * tq + col)
        keep = (diff >= 0) & (diff < WINDOW) & (jt >= 0)
        s = _dot_nt(q, k_ref[0, 0, pl.ds(k0, tq), :]).reshape(Q_PER_KV, tq, tq)
        scores.append(jnp.where(keep[None], s, NEG).reshape(rows, tq))
        starts.append(k0)
    m = functools.reduce(jnp.maximum, [jnp.max(s, axis=-1, keepdims=True) for s in scores])
    acc = jnp.zeros((rows, LANES), F32)
    for s, k0 in zip(scores, starts):
        acc = acc + _dot(jnp.exp(s - m).astype(BF16), v_ref[0, 0, pl.ds(k0, tq), :])
    o = (acc / acc[:, HEAD_DIM:HEAD_DIM + 1]).reshape(Q_PER_KV, tq, LANES)
    gate = gate_ref[0, 0]
    for r in range(Q_PER_KV):
        o_ref[0, r] = (o[r] * gate[:, 2 * Q_PER_KV + r:2 * Q_PER_KV + r + 1]).astype(BF16)


def _win_attn(q, k, v, gates, tq=256):
    b, _, s, _ = q.shape
    assert WINDOW % tq == 0
    qspec = pl.BlockSpec((1, Q_PER_KV, tq, LANES), lambda bi, g, i: (bi, g, i, 0))
    kspec = pl.BlockSpec((1, 1, s, LANES), lambda bi, g, i: (bi, g, 0, 0))
    return pl.pallas_call(
        _win_body,
        grid=(b, NSA_KV_GROUPS, s // tq),
        in_specs=[qspec, kspec, kspec, pl.BlockSpec((1, 1, tq, LANES), lambda bi, g, i: (bi, g, i, 0))],
        out_specs=qspec,
        out_shape=jax.ShapeDtypeStruct((b, NSA_HEADS, s, LANES), BF16),
        compiler_params=_params(("parallel", "parallel", "arbitrary")),
        name="win_attn",
    )(q, k, v, gates)


def _ssm_disc_body(lr_ref, li_ref, ls_ref, br_ref, bi_ref, ar_ref, ai_ref, bbr_ref, bbi_ref):
    lr, li = lr_ref[...], li_ref[...]
    step = jnp.exp(ls_ref[...])
    mag = jnp.exp(lr * step)
    ar = mag * jnp.cos(li * step)
    ai = mag * jnp.sin(li * step)
    den = lr * lr + li * li
    cr = ((ar - 1.0) * lr + ai * li) / den
    ci = (ai * lr - (ar - 1.0) * li) / den
    br, bi = br_ref[...], bi_ref[...]
    ar_ref[...] = jnp.broadcast_to(ar, ar_ref.shape)
    ai_ref[...] = jnp.broadcast_to(ai, ai_ref.shape)
    bbr_ref[...] = cr * br - ci * bi
    bbi_ref[...] = cr * bi + ci * br


def _ssm_disc(lam_re, lam_im, log_step, b_re, b_im):
    row = lambda a: a.reshape(1, SSM_STATES)
    chan = lambda a: a.transpose(2, 0, 1).reshape(SSM_GROUP, SSM_STATES)
    ls = jnp.broadcast_to(log_step[:, None], (SSM_GROUPS, SSM_STATE))
    return pl.pallas_call(
        _ssm_disc_body,
        out_shape=[jax.ShapeDtypeStruct((SUBLANES, SSM_STATES), F32)] * 2
        + [jax.ShapeDtypeStruct((SSM_GROUP, SSM_STATES), F32)] * 2,
        name="ssm_disc",
    )(row(lam_re), row(lam_im), row(ls), chan(b_re), chan(b_im))


def _group_diag_in(w):
    w = w.reshape(SSM_GROUP, SSM_GROUPS, SSM_STATE)
    eye = jnp.eye(SSM_GROUPS, dtype=w.dtype)
    return (w[None] * eye[:, None, :, None]).reshape(SSM_WIDTH, SSM_STATES)


def _group_diag_out(c):
    eye = jnp.eye(SSM_GROUPS, dtype=c.dtype)
    return (c.transpose(0, 2, 1)[:, :, None, :] * eye[:, None, :, None]).reshape(SSM_STATES, SSM_WIDTH)


SCAN_COLS = 512


def _ssm_scan_body(u_ref, ar_ref, ai_ref, wbr_ref, wbi_ref, wcr_ref, wci_ref, d_ref, o_ref,
                   xr_ref, xi_ref, vr_ref, vi_ref):
    steps = u_ref.shape[0]
    rows = steps * SUBLANES

    @pl.when(pl.program_id(0) == 0)
    def _():
        xr_ref[...] = jnp.zeros_like(xr_ref)
        xi_ref[...] = jnp.zeros_like(xi_ref)

    u = u_ref[...].reshape(rows, SSM_WIDTH)
    ub = u.astype(BF16)
    vr_ref[...] = _dot(ub, wbr_ref[...])
    vi_ref[...] = _dot(ub, wbi_ref[...])
    for cb in range(SSM_STATES // SCAN_COLS):
        cs = slice(cb * SCAN_COLS, (cb + 1) * SCAN_COLS)
        ar, ai = ar_ref[:, cs], ai_ref[:, cs]

        def step(t, carry):
            xr, xi = carry
            r0 = pl.multiple_of(t * SUBLANES, SUBLANES)
            nr = ar * xr - ai * xi + vr_ref[pl.ds(r0, SUBLANES), cs]
            ni = ar * xi + ai * xr + vi_ref[pl.ds(r0, SUBLANES), cs]
            vr_ref[pl.ds(r0, SUBLANES), cs] = nr
            vi_ref[pl.ds(r0, SUBLANES), cs] = ni
            return nr, ni

        xr, xi = lax.fori_loop(0, steps, step, (xr_ref[:, cs], xi_ref[:, cs]), unroll=8)
        xr_ref[:, cs] = xr
        xi_ref[:, cs] = xi
    y = (_dot(vr_ref[...].astype(BF16), wcr_ref[...]) - _dot(vi_ref[...].astype(BF16), wci_ref[...])
         + d_ref[...] * u)
    o_ref[...] = jax.nn.gelu(y).reshape(steps, SUBLANES, SSM_WIDTH)


def _ssm_scan(u_tm, ar, ai, wbr, wbi, wcr, wci, d, steps=64):
    s, b, _ = u_tm.shape
    assert b == SUBLANES
    rows = steps * SUBLANES
    return pl.pallas_call(
        _ssm_scan_body,
        grid=(s // steps,),
        in_specs=[pl.BlockSpec((steps, b, SSM_WIDTH), lambda i: (i, 0, 0)),
                  _const_spec((SUBLANES, SSM_STATES)), _const_spec((SUBLANES, SSM_STATES)),
                  _const_spec((SSM_WIDTH, SSM_STATES)), _const_spec((SSM_WIDTH, SSM_STATES)),
                  _const_spec((SSM_STATES, SSM_WIDTH)), _const_spec((SSM_STATES, SSM_WIDTH)),
                  _const_spec((1, SSM_WIDTH))],
        out_specs=pl.BlockSpec((steps, b, SSM_WIDTH), lambda i: (i, 0, 0)),
        out_shape=jax.ShapeDtypeStruct((s, b, SSM_WIDTH), F32),
        scratch_shapes=[pltpu.VMEM((SUBLANES, SSM_STATES), F32), pltpu.VMEM((SUBLANES, SSM_STATES), F32),
                        pltpu.VMEM((rows, SSM_STATES), F32), pltpu.VMEM((rows, SSM_STATES), F32)],
        compiler_params=_params(("arbitrary",)),
        name="ssm_scan",
    )(u_tm, ar, ai, wbr, wbi, wcr, wci, d)


def _merge_body(h_ref, g_ref, wgate_ref, oc_ref, os_ref, ow_ref, wn_ref, gy_ref, wglu_ref, wout_ref, o_ref):
    h = h_ref[0]
    hb = _rms(h, g_ref[...]).astype(BF16)
    gates = jax.nn.sigmoid(_dot(hb, wgate_ref[...]))
    y_nsa = jnp.zeros_like(h)
    for hd in range(NSA_HEADS):
        o = oc_ref[0, hd].astype(F32) + os_ref[0, hd].astype(F32) + ow_ref[0, hd].astype(F32)
        y_nsa = y_nsa + _dot(o.astype(BF16), wn_ref[hd])
    hg = _dot(gy_ref[0], wglu_ref[...])
    y_ssm = hg[:, :D_MODEL] * jax.nn.sigmoid(hg[:, D_MODEL:])
    merged = gates[:, :D_MODEL] * y_nsa + gates[:, D_MODEL:] * y_ssm
    o_ref[0] = h + _dot(merged.astype(BF16), wout_ref[...])


def _merge(h, mix_norm, w_gates, o_cmp, o_slc, o_win, w_nsa_proj, gy, glu_w, w_out, ts=512):
    b, s, _ = h.shape
    wn = jnp.pad(w_nsa_proj.reshape(NSA_HEADS, HEAD_DIM, D_MODEL), ((0, 0), (0, LANES - HEAD_DIM), (0, 0)))
    row_spec = lambda w: pl.BlockSpec((1, ts, w), lambda bi, i: (bi, i, 0))
    ospec = pl.BlockSpec((1, NSA_HEADS, ts, LANES), lambda bi, i: (bi, 0, i, 0))
    return pl.pallas_call(
        _merge_body,
        grid=(b, s // ts),
        in_specs=[row_spec(D_MODEL), _const_spec((1, D_MODEL)), _const_spec((D_MODEL, 2 * D_MODEL)),
                  ospec, ospec, ospec, _const_spec((NSA_HEADS, LANES, D_MODEL)),
                  row_spec(SSM_WIDTH), _const_spec((SSM_WIDTH, 2 * D_MODEL)), _const_spec((D_MODEL, D_MODEL))],
        out_specs=row_spec(D_MODEL),
        out_shape=jax.ShapeDtypeStruct((b, s, D_MODEL), F32),
        compiler_params=_params(("parallel", "parallel")),
        name="merge",
    )(h, mix_norm.reshape(1, D_MODEL), w_gates.astype(BF16), o_cmp, o_slc, o_win, wn.astype(BF16),
      gy, glu_w.astype(BF16), w_out.astype(BF16))


def _layer(h, ffn1_norm, ffn1_w_gate, ffn1_w_up, ffn1_w_down, mix_norm, w_in, q_norm,
           k_norm_cmp, k_norm_slc, k_norm_win, cmp_pos_k, cmp_pos_v, cmp_k_w1, cmp_k_w2,
           cmp_v_w1, cmp_v_w2, w_nsa_proj, ssm_lambda_re, ssm_lambda_im, ssm_log_step,
           ssm_b_re, ssm_b_im, ssm_c_re, ssm_c_im, ssm_d, ssm_glu_w, w_out,
           ffn2_norm, ffn2_w_gate, ffn2_w_up, ffn2_w_down):
    b, s, _ = h.shape
    h = _ffn(h.reshape(b * s, D_MODEL), ffn1_norm, ffn1_w_gate, ffn1_w_up, ffn1_w_down).reshape(b, s, D_MODEL)
    q, kvc, ks, vs, kw, vw, gates, u = _proj(h, mix_norm, w_in[:, :P_END_SRC], q_norm, k_norm_slc, k_norm_win)
    kc, vc = _compress(kvc, cmp_pos_k, cmp_pos_v, cmp_k_w1, cmp_v_w1, cmp_k_w2, cmp_v_w2, k_norm_cmp)
    o_cmp, qs = _cmp_select(q, kc, vc, gates)
    o_slc = _slc_attn(qs, ks, vs, gates)
    o_win = _win_attn(q, kw, vw, gates)
    ar, ai, bbr, bbi = _ssm_disc(ssm_lambda_re, ssm_lambda_im, ssm_log_step, ssm_b_re, ssm_b_im)
    gy = _ssm_scan(u.transpose(1, 0, 2), ar, ai,
                   _group_diag_in(bbr).astype(BF16), _group_diag_in(bbi).astype(BF16),
                   _group_diag_out(ssm_c_re).astype(BF16), _group_diag_out(ssm_c_im).astype(BF16),
                   ssm_d.reshape(1, SSM_WIDTH))
    h = _merge(h, mix_norm, w_in[:, P_END_SRC:], o_cmp, o_slc, o_win, w_nsa_proj,
               gy.transpose(1, 0, 2).astype(BF16), ssm_glu_w, w_out)
    h = _ffn(h.reshape(b * s, D_MODEL), ffn2_norm, ffn2_w_gate, ffn2_w_up, ffn2_w_down)
    return h.reshape(b, s, D_MODEL)


P_END_SRC = NSA_WIDTH + 6 * KV_WIDTH + 3 * NSA_HEADS + SSM_WIDTH


def kernel(x, ffn1_norm, ffn1_w_gate, ffn1_w_up, ffn1_w_down, mix_norm, w_in, q_norm, k_norm_cmp, k_norm_slc, k_norm_win, cmp_pos_k, cmp_pos_v, cmp_k_w1, cmp_k_w2, cmp_v_w1, cmp_v_w2, w_nsa_proj, ssm_lambda_re, ssm_lambda_im, ssm_log_step, ssm_b_re, ssm_b_im, ssm_c_re, ssm_c_im, ssm_d, ssm_glu_w, w_out, ffn2_norm, ffn2_w_gate, ffn2_w_up, ffn2_w_down):
    params = (ffn1_norm, ffn1_w_gate, ffn1_w_up, ffn1_w_down, mix_norm, w_in, q_norm, k_norm_cmp, k_norm_slc,
              k_norm_win, cmp_pos_k, cmp_pos_v, cmp_k_w1, cmp_k_w2, cmp_v_w1, cmp_v_w2, w_nsa_proj,
              ssm_lambda_re, ssm_lambda_im, ssm_log_step, ssm_b_re, ssm_b_im, ssm_c_re, ssm_c_im, ssm_d,
              ssm_glu_w, w_out, ffn2_norm, ffn2_w_gate, ffn2_w_up, ffn2_w_down)
    h = x.astype(F32)
    for layer in range(ffn1_norm.shape[0]):
        h = _layer(h, *[p[layer] for p in params])
    return h.astype(x.dtype)
```

```python
import functools
import math

import jax
import jax.numpy as jnp
from jax import lax
from jax.experimental import pallas as pl
from jax.experimental.pallas import tpu as pltpu

F32 = jnp.float32
BF16 = jnp.bfloat16

D_MODEL = 1024
NSA_HEADS = 8
NSA_KV_GROUPS = 2
HEAD_DIM = 64
Q_PER_KV = NSA_HEADS // NSA_KV_GROUPS
NSA_WIDTH = NSA_HEADS * HEAD_DIM
KV_WIDTH = NSA_KV_GROUPS * HEAD_DIM
CMP_BLOCK = 32
CMP_STRIDE = 16
CMP_HIDDEN = 256
SLC_BLOCK = 64
SLC_TOPK = 16
WINDOW = 512
FORCE_BONUS = 1000.0
SSM_GROUP = 16
SSM_GROUPS = 32
SSM_STATE = 64
SSM_WIDTH = SSM_GROUPS * SSM_GROUP
SSM_STATES = SSM_GROUPS * SSM_STATE
D_FF = 2816
RMS_EPS = 1e-6
NEG = -1e30

LANES = 128
SUBLANES = 8
VMEM_LIMIT = 56 * 1024 * 1024

P_Q = 0
P_KVC = P_Q + NSA_HEADS * LANES
P_KS = P_KVC + 2 * KV_WIDTH
P_VS = P_KS + NSA_KV_GROUPS * LANES
P_KW = P_VS + NSA_KV_GROUPS * LANES
P_VW = P_KW + NSA_KV_GROUPS * LANES
P_GATE = P_VW + NSA_KV_GROUPS * LANES
P_U = P_GATE + NSA_KV_GROUPS * LANES
P_END = P_U + SSM_WIDTH


def _params(sem, vmem=VMEM_LIMIT):
    return pltpu.CompilerParams(dimension_semantics=sem, vmem_limit_bytes=vmem)


def _const_spec(shape):
    n = len(shape)
    return pl.BlockSpec(shape, lambda *_: (0,) * n, pipeline_mode=pl.Buffered(1))


def _rms(x, g):
    return x * lax.rsqrt(jnp.mean(x * x, axis=-1, keepdims=True) + RMS_EPS) * g


def _slot_rms(x, g):
    ms = jnp.sum(x * x, axis=-1, keepdims=True) * (1.0 / HEAD_DIM)
    return x * lax.rsqrt(ms + RMS_EPS) * g


def _dot(a, b):
    return jnp.dot(a, b, preferred_element_type=F32)


def _dot_nt(a, b):
    return lax.dot_general(a, b, (((1,), (1,)), ((), ())), preferred_element_type=F32)


MXU_DIM = 256
FF_SPLITS = (0, (D_FF // 2 + MXU_DIM - 1) // MXU_DIM * MXU_DIM, D_FF)


def _ffn_body(x_ref, g_ref, wg_ref, wu_ref, wd_ref, o_ref):
    x = x_ref[...]
    hb = _rms(x, g_ref[...]).astype(BF16)
    acc = jnp.zeros_like(x)
    for c in range(len(FF_SPLITS) - 1):
        sl = slice(FF_SPLITS[c], FF_SPLITS[c + 1])
        gate = _dot(hb, wg_ref[:, sl])
        up = _dot(hb, wu_ref[:, sl])
        act = (gate * jax.nn.sigmoid(gate) * up).astype(BF16)
        acc = acc + _dot(act, wd_ref[sl, :])
    o_ref[...] = x + 0.5 * acc


def _ffn(x2, g, wg, wu, wd, tm=512):
    t = x2.shape[0]
    return pl.pallas_call(
        _ffn_body,
        grid=(t // tm,),
        in_specs=[pl.BlockSpec((tm, D_MODEL), lambda i: (i, 0)),
                  _const_spec((1, D_MODEL)),
                  _const_spec((D_MODEL, D_FF)),
                  _const_spec((D_MODEL, D_FF)),
                  _const_spec((D_FF, D_MODEL))],
        out_specs=pl.BlockSpec((tm, D_MODEL), lambda i: (i, 0)),
        out_shape=jax.ShapeDtypeStruct((t, D_MODEL), F32),
        compiler_params=_params(("parallel",)),
        name="ffn",
    )(x2, g.reshape(1, D_MODEL), wg.astype(BF16), wu.astype(BF16), wd.astype(BF16))


def _proj_body(h_ref, g_ref, w_ref, qn_ref, ksn_ref, kwn_ref,
               q_ref, kvc_ref, ks_ref, vs_ref, kw_ref, vw_ref, gate_ref, u_ref):
    ts = h_ref.shape[1]
    hb = _rms(h_ref[0], g_ref[...]).astype(BF16)
    p = _dot(hb, w_ref[...])
    lane = lax.broadcasted_iota(jnp.int32, (ts, LANES), 1)
    tok = lax.broadcasted_iota(jnp.int32, (ts, LANES), 0) + pl.program_id(1) * ts
    low = lane < HEAD_DIM
    onehot = jnp.where(lane - HEAD_DIM == tok // SLC_BLOCK, 1.0, 0.0)
    qscale = HEAD_DIM ** -0.5 * math.log2(math.e)
    for h in range(NSA_HEADS):
        qh = p[:, P_Q + h * LANES:P_Q + (h + 1) * LANES]
        q_ref[0, h] = (_slot_rms(qh, qn_ref[...]) * qscale).astype(BF16)
    kvc_ref[0] = p[:, P_KVC:P_KS]
    for g in range(NSA_KV_GROUPS):
        sl = lambda base: slice(base + g * LANES, base + (g + 1) * LANES)
        ks = _slot_rms(p[:, sl(P_KS)], ksn_ref[...])
        ks_ref[0, g] = jnp.where(low, ks, onehot).astype(BF16)
        vs_ref[0, g] = jnp.where(low, p[:, sl(P_VS)], 1.0).astype(BF16)
        kw_ref[0, g] = _slot_rms(p[:, sl(P_KW)], kwn_ref[...]).astype(BF16)
        vw_ref[0, g] = jnp.where(low, p[:, sl(P_VW)], 1.0).astype(BF16)
        gate_ref[0, g] = jax.nn.sigmoid(p[:, sl(P_GATE)])
    u_ref[0] = p[:, P_U:P_END]


def _slot_pad(w, n):
    d = w.shape[0]
    return jnp.pad(w.reshape(d, n, HEAD_DIM), ((0, 0), (0, 0), (0, LANES - HEAD_DIM))).reshape(d, n * LANES)


def _slot_gain(g):
    return jnp.pad(g, (0, LANES - HEAD_DIM)).reshape(1, LANES)


def _proj_weight(w_in):
    cuts = [NSA_WIDTH] + [KV_WIDTH] * 6 + [3 * NSA_HEADS, SSM_WIDTH]
    offs = [0]
    for c in cuts:
        offs.append(offs[-1] + c)
    seg = lambda i: w_in[:, offs[i]:offs[i + 1]]
    gate = seg(7).reshape(D_MODEL, 3, NSA_KV_GROUPS, Q_PER_KV).transpose(0, 2, 1, 3)
    gate = gate.reshape(D_MODEL, NSA_KV_GROUPS, 3 * Q_PER_KV)
    gate = jnp.pad(gate, ((0, 0), (0, 0), (0, LANES - 3 * Q_PER_KV))).reshape(D_MODEL, NSA_KV_GROUPS * LANES)
    cols = [_slot_pad(seg(0), NSA_HEADS), seg(1), seg(2)]
    cols += [_slot_pad(seg(i), NSA_KV_GROUPS) for i in (3, 4, 5, 6)]
    cols += [gate, seg(8)]
    return jnp.concatenate(cols, axis=1).astype(BF16)


def _proj(h, mix_norm, w_in, q_norm, k_norm_slc, k_norm_win, ts=512):
    b, s, _ = h.shape
    hm = lambda n: jax.ShapeDtypeStruct((b, n, s, LANES), BF16)
    hm_spec = lambda n: pl.BlockSpec((1, n, ts, LANES), lambda bi, i: (bi, 0, i, 0))
    row_spec = lambda w: pl.BlockSpec((1, ts, w), lambda bi, i: (bi, i, 0))
    return pl.pallas_call(
        _proj_body,
        grid=(b, s // ts),
        in_specs=[row_spec(D_MODEL), _const_spec((1, D_MODEL)), _const_spec((D_MODEL, P_END)),
                  _const_spec((1, LANES)), _const_spec((1, LANES)), _const_spec((1, LANES))],
        out_specs=[hm_spec(NSA_HEADS), row_spec(2 * KV_WIDTH),
                   hm_spec(NSA_KV_GROUPS), hm_spec(NSA_KV_GROUPS), hm_spec(NSA_KV_GROUPS), hm_spec(NSA_KV_GROUPS),
                   hm_spec(NSA_KV_GROUPS), row_spec(SSM_WIDTH)],
        out_shape=[hm(NSA_HEADS), jax.ShapeDtypeStruct((b, s, 2 * KV_WIDTH), F32),
                   hm(NSA_KV_GROUPS), hm(NSA_KV_GROUPS), hm(NSA_KV_GROUPS), hm(NSA_KV_GROUPS),
                   jax.ShapeDtypeStruct((b, NSA_KV_GROUPS, s, LANES), F32),
                   jax.ShapeDtypeStruct((b, s, SSM_WIDTH), F32)],
        compiler_params=_params(("parallel", "parallel")),
        name="proj",
    )(h, mix_norm.reshape(1, D_MODEL), _proj_weight(w_in),
      _slot_gain(q_norm), _slot_gain(k_norm_slc), _slot_gain(k_norm_win))


CMP_ROW = CMP_STRIDE * 2 * KV_WIDTH
CMP_HID_ALL = 4 * CMP_HIDDEN


def _compress_body(c_ref, wlo_ref, whi_ref, pk_ref, pv_ref, w1k_ref, w1v_ref, w2_ref, kn_ref,
                   kc_ref, vc_ref):
    nch = c_ref.shape[1]
    c = c_ref[0].astype(BF16)
    first = _dot(c, wlo_ref[...])
    second = _dot(c, whi_ref[...])
    second = pltpu.roll(second, nch - 1, 0)
    rows8 = lambda r: jnp.broadcast_to(r[...].astype(BF16), (SUBLANES, r.shape[1]))
    bk = _dot(rows8(pk_ref), w1k_ref[...])[0:1]
    bv = _dot(rows8(pv_ref), w1v_ref[...])[0:1]
    bias = jnp.concatenate([bk, bk, bv, bv], axis=1)
    hid = jax.nn.gelu(first + second + bias).astype(BF16)
    out = _dot(hid, w2_ref[...])
    lane = lax.broadcasted_iota(jnp.int32, (nch, LANES), 1)
    low = lane < HEAD_DIM
    for g in range(NSA_KV_GROUPS):
        kc_ref[0, g] = _slot_rms(out[:, g * LANES:(g + 1) * LANES], kn_ref[...]).astype(BF16)
        v = out[:, (NSA_KV_GROUPS + g) * LANES:(NSA_KV_GROUPS + g + 1) * LANES]
        vc_ref[0, g] = jnp.where(low, v, 1.0).astype(BF16)


def _compress_weights(w1k, w1v, w2k, w2v):
    def half(w1, lo):
        w = w1[lo * CMP_STRIDE * HEAD_DIM:(lo + 1) * CMP_STRIDE * HEAD_DIM].reshape(CMP_STRIDE, HEAD_DIM, CMP_HIDDEN)
        return w
    eye4 = jnp.eye(4, dtype=F32)
    def build(lo):
        wk, wv = half(w1k, lo), half(w1v, lo)
        w = jnp.stack([wk, wk, wv, wv], axis=1)
        w = w[:, :, :, None, :] * eye4[None, :, None, :, None]
        return w.reshape(CMP_ROW, CMP_HID_ALL).astype(BF16)
    w2 = jnp.stack([w2k, w2k, w2v, w2v], axis=0)
    w2 = jnp.pad(w2, ((0, 0), (0, 0), (0, LANES - HEAD_DIM)))
    w2 = w2[:, :, None, :] * eye4[:, None, :, None]
    return build(0), build(1), w2.reshape(CMP_HID_ALL, 4 * LANES).astype(BF16)


def _compress(kvc, cmp_pos_k, cmp_pos_v, w1k, w1v, w2k, w2v, k_norm_cmp):
    b, s, _ = kvc.shape
    nch = s // CMP_STRIDE
    wlo, whi, w2 = _compress_weights(w1k, w1v, w2k, w2v)
    flat = CMP_BLOCK * HEAD_DIM
    out = jax.ShapeDtypeStruct((b, NSA_KV_GROUPS, nch, LANES), BF16)
    out_spec = pl.BlockSpec((1, NSA_KV_GROUPS, nch, LANES), lambda bi: (bi, 0, 0, 0))
    return pl.pallas_call(
        _compress_body,
        grid=(b,),
        in_specs=[pl.BlockSpec((1, nch, CMP_ROW), lambda bi: (bi, 0, 0)),
                  _const_spec((CMP_ROW, CMP_HID_ALL)), _const_spec((CMP_ROW, CMP_HID_ALL)),
                  _const_spec((1, flat)), _const_spec((1, flat)),
                  _const_spec((flat, CMP_HIDDEN)), _const_spec((flat, CMP_HIDDEN)),
                  _const_spec((CMP_HID_ALL, 4 * LANES)), _const_spec((1, LANES))],
        out_specs=[out_spec, out_spec],
        out_shape=[out, out],
        compiler_params=_params(("parallel",)),
        name="compress",
    )(kvc.reshape(b, nch, CMP_ROW), wlo, whi, cmp_pos_k.reshape(1, flat), cmp_pos_v.reshape(1, flat),
      w1k.astype(BF16), w1v.astype(BF16), w2, _slot_gain(k_norm_cmp))


def _masked_softmax(s, mask, axis):
    s = jnp.where(mask, s, NEG)
    m = jnp.max(s, axis=axis, keepdims=True)
    e = jnp.where(mask, jnp.exp2(s - m), 0.0)
    den = jnp.sum(e, axis=axis, keepdims=True)
    return e / jnp.where(den > 0, den, 1.0)


def _topk_rows(score, k):
    n, t = score.shape
    gone = -3e38

    def peel(_, carry):
        work, cnt, thr = carry
        m = jnp.max(work, axis=0, keepdims=True)
        hit = work == m
        thr = jnp.where(cnt < k, m, thr)
        cnt = cnt + jnp.sum(jnp.where(hit, 1.0, 0.0), axis=0, keepdims=True)
        return jnp.where(hit, gone, work), cnt, thr

    init = (score, jnp.zeros((1, t), F32), jnp.full((1, t), gone, F32))
    _, _, thr = lax.fori_loop(0, k, peel, init, unroll=True)
    above = jnp.where(score > thr, 1.0, 0.0)
    tied = jnp.where(score == thr, 1.0, 0.0)
    room = k - jnp.sum(above, axis=0, keepdims=True)
    lower = lax.broadcasted_iota(jnp.int32, (n, n), 1) < lax.broadcasted_iota(jnp.int32, (n, n), 0)
    tied_before = _dot(jnp.where(lower, 1.0, 0.0).astype(BF16), tied.astype(BF16))
    return jnp.maximum(above, tied * jnp.where(tied_before < room, 1.0, 0.0))


def _cmp_select_body(q_ref, kc_ref, vc_ref, gate_ref, o_ref, qs_ref, *, ns):
    tq = q_ref.shape[2]
    nch = kc_ref.shape[2]
    t0 = pl.program_id(2) * tq
    q = q_ref[0].reshape(Q_PER_KV * tq, LANES)
    s = _dot_nt(q, kc_ref[0, 0]).reshape(Q_PER_KV, tq, nch)
    tok = t0 + lax.broadcasted_iota(jnp.int32, (tq, nch), 0)
    cid = lax.broadcasted_iota(jnp.int32, (tq, nch), 1)
    p = _masked_softmax(s, (cid * CMP_STRIDE + CMP_BLOCK - 1 <= tok)[None], axis=-1)
    o = _dot(p.reshape(Q_PER_KV * tq, nch).astype(BF16), vc_ref[0, 0]).reshape(Q_PER_KV, tq, LANES)
    gate = gate_ref[0, 0]
    for r in range(Q_PER_KV):
        o_ref[0, r] = (o[r] * gate[:, r:r + 1]).astype(BF16)
    psum = p[0] + p[1] + p[2] + p[3]
    cb = lax.broadcasted_iota(jnp.int32, (nch, LANES), 0)
    jb = lax.broadcasted_iota(jnp.int32, (nch, LANES), 1)
    overlap = jnp.where((cb * CMP_STRIDE < (jb + 1) * SLC_BLOCK) & (cb * CMP_STRIDE + CMP_BLOCK > jb * SLC_BLOCK), 1.0, 0.0)
    imp = _dot(psum, overlap).T[:ns]
    j = lax.broadcasted_iota(jnp.int32, (ns, tq), 0)
    qblk = (t0 + lax.broadcasted_iota(jnp.int32, (ns, tq), 1)) // SLC_BLOCK
    force = (j == 0) | (j == qblk) | (j == qblk - 1)
    score = jnp.where(j <= qblk, imp + FORCE_BONUS * jnp.where(force, 1.0, 0.0), NEG)
    sel_bias = jnp.where(_topk_rows(score, SLC_TOPK) > 0, 0.0, NEG)
    pieces = [jnp.zeros((HEAD_DIM, tq), F32), sel_bias]
    if ns < LANES - HEAD_DIM:
        pieces.append(jnp.zeros((LANES - HEAD_DIM - ns, tq), F32))
    bias = jnp.concatenate(pieces, axis=0).T
    for r in range(Q_PER_KV):
        qs_ref[0, r] = (q_ref[0, r].astype(F32) + bias).astype(BF16)


def _cmp_select(q, kc, vc, gates, tq=256):
    b, _, s, _ = q.shape
    nch = kc.shape[2]
    ns = s // SLC_BLOCK
    assert ns <= LANES - HEAD_DIM
    qspec = pl.BlockSpec((1, Q_PER_KV, tq, LANES), lambda bi, g, i: (bi, g, i, 0))
    cspec = pl.BlockSpec((1, 1, nch, LANES), lambda bi, g, i: (bi, g, 0, 0))
    out = jax.ShapeDtypeStruct((b, NSA_HEADS, s, LANES), BF16)
    return pl.pallas_call(
        functools.partial(_cmp_select_body, ns=ns),
        grid=(b, NSA_KV_GROUPS, s // tq),
        in_specs=[qspec, cspec, cspec, pl.BlockSpec((1, 1, tq, LANES), lambda bi, g, i: (bi, g, i, 0))],
        out_specs=[qspec, qspec],
        out_shape=[out, out],
        compiler_params=_params(("parallel", "parallel", "parallel")),
        name="cmp_select",
    )(q, kc, vc, gates)


def _slc_body(q_ref, k_ref, v_ref, gate_ref, o_ref, p_ref, m_ref, acc_ref):
    tq = q_ref.shape[2]
    i = pl.program_id(2)
    n_lane_tiles = tq // LANES

    def scores(r, j):
        k0 = pl.multiple_of(j * tq, tq)
        return _dot_nt(q_ref[0, r], k_ref[0, 0, pl.ds(k0, tq), :]).astype(BF16)

    def pending_values(r, j):
        k0 = pl.multiple_of(j * tq, tq)
        return _dot(p_ref[r], v_ref[0, 0, pl.ds(k0, tq), :])

    def row_max(s):
        part = functools.reduce(jnp.maximum, [s[:, c * LANES:(c + 1) * LANES] for c in range(n_lane_tiles)])
        return jnp.broadcast_to(jnp.max(part, axis=-1, keepdims=True), (tq, LANES))

    def probabilities(s, m):
        return jnp.concatenate([jnp.exp2(s[:, c * LANES:(c + 1) * LANES] - m) for c in range(n_lane_tiles)], axis=1)

    keep = lax.broadcasted_iota(jnp.int32, (tq, tq), 1) <= lax.broadcasted_iota(jnp.int32, (tq, tq), 0)
    for r in range(Q_PER_KV):
        s = jnp.where(keep, scores(r, i), NEG)
        m = row_max(s)
        m_ref[r] = m
        p_ref[r] = probabilities(s, m)
        acc_ref[r] = jnp.zeros((tq, LANES), F32)

    def step(j, carry):
        j_pend = jnp.where(j == 0, i, j - 1)
        for r in range(Q_PER_KV):
            pv = pending_values(r, j_pend)
            s = scores(r, j)
            m = m_ref[r]
            m_new = jnp.maximum(m, row_max(s))
            p_ref[r] = probabilities(s, m_new)
            alpha = jnp.exp2(m.astype(F32) - m_new.astype(F32))
            acc_ref[r] = alpha * (acc_ref[r] + pv)
            m_ref[r] = m_new
        return carry

    lax.fori_loop(0, i, step, 0)
    gate = gate_ref[0, 0]
    for r in range(Q_PER_KV):
        acc = acc_ref[r] + pending_values(r, jnp.where(i == 0, 0, i - 1))
        o = acc / acc[:, HEAD_DIM:HEAD_DIM + 1]
        o_ref[0, r] = (o * gate[:, Q_PER_KV + r:Q_PER_KV + r + 1]).astype(BF16)


def _slc_attn(qs, k, v, gates, tq=512):
    b, _, s, _ = qs.shape
    qspec = pl.BlockSpec((1, Q_PER_KV, tq, LANES), lambda bi, g, i: (bi, g, i, 0))
    kspec = pl.BlockSpec((1, 1, s, LANES), lambda bi, g, i: (bi, g, 0, 0))
    return pl.pallas_call(
        _slc_body,
        grid=(b, NSA_KV_GROUPS, s // tq),
        in_specs=[qspec, kspec, kspec, pl.BlockSpec((1, 1, tq, LANES), lambda bi, g, i: (bi, g, i, 0))],
        out_specs=qspec,
        out_shape=jax.ShapeDtypeStruct((b, NSA_HEADS, s, LANES), BF16),
        scratch_shapes=[pltpu.VMEM((Q_PER_KV, tq, tq), BF16), pltpu.VMEM((Q_PER_KV, tq, LANES), BF16),
                        pltpu.VMEM((Q_PER_KV, tq, LANES), F32)],
        compiler_params=_params(("parallel", "parallel", "arbitrary")),
        name="slc_attn",
    )(qs, k, v, gates)


def _win_body(q_ref, k_ref, v_ref, gate_ref, o_ref):
    tq = q_ref.shape[2]
    rows = Q_PER_KV * tq
    nprev = WINDOW // tq
    i = pl.program_id(2)
    q = q_ref[0].reshape(rows, LANES)
    qpos = i * tq + lax.broadcasted_iota(jnp.int32, (tq, tq), 0)
    col = lax.broadcasted_iota(jnp.int32, (tq, tq), 1)
    scores, starts = [], []
    for d in range(nprev + 1):
        jt = i - nprev + d
        k0 = pl.multiple_of(jnp.maximum(jt, 0) * tq, tq)
        diff = qpos - (jt * tq + col)
        keep = (diff >= 0) & (diff < WINDOW) & (jt >= 0)
        s = _dot_nt(q, k_ref[0, 0, pl.ds(k0, tq), :]).reshape(Q_PER_KV, tq, tq)
        scores.append(jnp.where(keep[None], s, NEG).reshape(rows, tq))
        starts.append(k0)
    m = functools.reduce(jnp.maximum, [jnp.max(s, axis=-1, keepdims=True) for s in scores])
    acc = jnp.zeros((rows, LANES), F32)
    for s, k0 in zip(scores, starts):
        acc = acc + _dot(jnp.exp2(s - m).astype(BF16), v_ref[0, 0, pl.ds(k0, tq), :])
    o = (acc / acc[:, HEAD_DIM:HEAD_DIM + 1]).reshape(Q_PER_KV, tq, LANES)
    gate = gate_ref[0, 0]
    for r in range(Q_PER_KV):
        o_ref[0, r] = (o[r] * gate[:, 2 * Q_PER_KV + r:2 * Q_PER_KV + r + 1]).astype(BF16)


def _win_attn(q, k, v, gates, tq=256):
    b, _, s, _ = q.shape
    assert WINDOW % tq == 0
    qspec = pl.BlockSpec((1, Q_PER_KV, tq, LANES), lambda bi, g, i: (bi, g, i, 0))
    kspec = pl.BlockSpec((1, 1, s, LANES), lambda bi, g, i: (bi, g, 0, 0))
    return pl.pallas_call(
        _win_body,
        grid=(b, NSA_KV_GROUPS, s // tq),
        in_specs=[qspec, kspec, kspec, pl.BlockSpec((1, 1, tq, LANES), lambda bi, g, i: (bi, g, i, 0))],
        out_specs=qspec,
        out_shape=jax.ShapeDtypeStruct((b, NSA_HEADS, s, LANES), BF16),
        compiler_params=_params(("parallel", "parallel", "arbitrary")),
        name="win_attn",
    )(q, k, v, gates)


def _ssm_disc_body(lr_ref, li_ref, ls_ref, br_ref, bi_ref, ar_ref, ai_ref, bbr_ref, bbi_ref):
    lr, li = lr_ref[...], li_ref[...]
    step = jnp.exp(ls_ref[...])
    mag = jnp.exp(lr * step)
    ar = mag * jnp.cos(li * step)
    ai = mag * jnp.sin(li * step)
    den = lr * lr + li * li
    cr = ((ar - 1.0) * lr + ai * li) / den
    ci = (ai * lr - (ar - 1.0) * li) / den
    br, bi = br_ref[...], bi_ref[...]
    ar_ref[...] = jnp.broadcast_to(ar, ar_ref.shape)
    ai_ref[...] = jnp.broadcast_to(ai, ai_ref.shape)
    bbr_ref[...] = cr * br - ci * bi
    bbi_ref[...] = cr * bi + ci * br


def _ssm_disc(lam_re, lam_im, log_step, b_re, b_im):
    row = lambda a: a.reshape(1, SSM_STATES)
    chan = lambda a: a.transpose(2, 0, 1).reshape(SSM_GROUP, SSM_STATES)
    ls = jnp.broadcast_to(log_step[:, None], (SSM_GROUPS, SSM_STATE))
    return pl.pallas_call(
        _ssm_disc_body,
        out_shape=[jax.ShapeDtypeStruct((SUBLANES, SSM_STATES), F32)] * 2
        + [jax.ShapeDtypeStruct((SSM_GROUP, SSM_STATES), F32)] * 2,
        name="ssm_disc",
    )(row(lam_re), row(lam_im), row(ls), chan(b_re), chan(b_im))


def _group_diag_in(w):
    w = w.reshape(SSM_GROUP, SSM_GROUPS, SSM_STATE)
    eye = jnp.eye(SSM_GROUPS, dtype=w.dtype)
    return (w[None] * eye[:, None, :, None]).reshape(SSM_WIDTH, SSM_STATES)


def _group_diag_out(c):
    eye = jnp.eye(SSM_GROUPS, dtype=c.dtype)
    return (c.transpose(0, 2, 1)[:, :, None, :] * eye[:, None, :, None]).reshape(SSM_STATES, SSM_WIDTH)


SCAN_COLS = 512


def _ssm_scan_body(u_ref, ar_ref, ai_ref, wbr_ref, wbi_ref, wcr_ref, wci_ref, d_ref, o_ref,
                   xr_ref, xi_ref, vr_ref, vi_ref):
    steps = u_ref.shape[0]
    rows = steps * SUBLANES

    @pl.when(pl.program_id(0) == 0)
    def _():
        xr_ref[...] = jnp.zeros_like(xr_ref)
        xi_ref[...] = jnp.zeros_like(xi_ref)

    u = u_ref[...].reshape(rows, SSM_WIDTH)
    ub = u.astype(BF16)
    vr_ref[...] = _dot(ub, wbr_ref[...])
    vi_ref[...] = _dot(ub, wbi_ref[...])
    for cb in range(SSM_STATES // SCAN_COLS):
        cs = slice(cb * SCAN_COLS, (cb + 1) * SCAN_COLS)
        ar, ai = ar_ref[:, cs], ai_ref[:, cs]

        def step(t, carry):
            xr, xi = carry
            r0 = pl.multiple_of(t * SUBLANES, SUBLANES)
            nr = ar * xr - ai * xi + vr_ref[pl.ds(r0, SUBLANES), cs]
            ni = ar * xi + ai * xr + vi_ref[pl.ds(r0, SUBLANES), cs]
            vr_ref[pl.ds(r0, SUBLANES), cs] = nr
            vi_ref[pl.ds(r0, SUBLANES), cs] = ni
            return nr, ni

        xr, xi = lax.fori_loop(0, steps, step, (xr_ref[:, cs], xi_ref[:, cs]), unroll=8)
        xr_ref[:, cs] = xr
        xi_ref[:, cs] = xi
    y = (_dot(vr_ref[...].astype(BF16), wcr_ref[...]) - _dot(vi_ref[...].astype(BF16), wci_ref[...])
         + d_ref[...] * u)
    o_ref[...] = jax.nn.gelu(y).reshape(steps, SUBLANES, SSM_WIDTH)


def _ssm_scan(u_tm, ar, ai, wbr, wbi, wcr, wci, d, steps=64):
    s, b, _ = u_tm.shape
    assert b == SUBLANES
    rows = steps * SUBLANES
    return pl.pallas_call(
        _ssm_scan_body,
        grid=(s // steps,),
        in_specs=[pl.BlockSpec((steps, b, SSM_WIDTH), lambda i: (i, 0, 0)),
                  _const_spec((SUBLANES, SSM_STATES)), _const_spec((SUBLANES, SSM_STATES)),
                  _const_spec((SSM_WIDTH, SSM_STATES)), _const_spec((SSM_WIDTH, SSM_STATES)),
                  _const_spec((SSM_STATES, SSM_WIDTH)), _const_spec((SSM_STATES, SSM_WIDTH)),
                  _const_spec((1, SSM_WIDTH))],
        out_specs=pl.BlockSpec((steps, b, SSM_WIDTH), lambda i: (i, 0, 0)),
        out_shape=jax.ShapeDtypeStruct((s, b, SSM_WIDTH), F32),
        scratch_shapes=[pltpu.VMEM((SUBLANES, SSM_STATES), F32), pltpu.VMEM((SUBLANES, SSM_STATES), F32),
                        pltpu.VMEM((rows, SSM_STATES), F32), pltpu.VMEM((rows, SSM_STATES), F32)],
        compiler_params=_params(("arbitrary",)),
        name="ssm_scan",
    )(u_tm, ar, ai, wbr, wbi, wcr, wci, d)


def _merge_body(h_ref, g_ref, wgate_ref, oc_ref, os_ref, ow_ref, wn_ref, gy_ref, wglu_ref, wout_ref, o_ref):
    h = h_ref[0]
    hb = _rms(h, g_ref[...]).astype(BF16)
    gates = jax.nn.sigmoid(_dot(hb, wgate_ref[...]))
    y_nsa = jnp.zeros_like(h)
    for hd in range(NSA_HEADS):
        o = oc_ref[0, hd].astype(F32) + os_ref[0, hd].astype(F32) + ow_ref[0, hd].astype(F32)
        y_nsa = y_nsa + _dot(o.astype(BF16), wn_ref[hd])
    hg = _dot(gy_ref[0], wglu_ref[...])
    y_ssm = hg[:, :D_MODEL] * jax.nn.sigmoid(hg[:, D_MODEL:])
    merged = gates[:, :D_MODEL] * y_nsa + gates[:, D_MODEL:] * y_ssm
    o_ref[0] = h + _dot(merged.astype(BF16), wout_ref[...])


def _merge(h, mix_norm, w_gates, o_cmp, o_slc, o_win, w_nsa_proj, gy, glu_w, w_out, ts=512):
    b, s, _ = h.shape
    wn = jnp.pad(w_nsa_proj.reshape(NSA_HEADS, HEAD_DIM, D_MODEL), ((0, 0), (0, LANES - HEAD_DIM), (0, 0)))
    row_spec = lambda w: pl.BlockSpec((1, ts, w), lambda bi, i: (bi, i, 0))
    ospec = pl.BlockSpec((1, NSA_HEADS, ts, LANES), lambda bi, i: (bi, 0, i, 0))
    return pl.pallas_call(
        _merge_body,
        grid=(b, s // ts),
        in_specs=[row_spec(D_MODEL), _const_spec((1, D_MODEL)), _const_spec((D_MODEL, 2 * D_MODEL)),
                  ospec, ospec, ospec, _const_spec((NSA_HEADS, LANES, D_MODEL)),
                  row_spec(SSM_WIDTH), _const_spec((SSM_WIDTH, 2 * D_MODEL)), _const_spec((D_MODEL, D_MODEL))],
        out_specs=row_spec(D_MODEL),
        out_shape=jax.ShapeDtypeStruct((b, s, D_MODEL), F32),
        compiler_params=_params(("parallel", "parallel")),
        name="merge",
    )(h, mix_norm.reshape(1, D_MODEL), w_gates.astype(BF16), o_cmp, o_slc, o_win, wn.astype(BF16),
      gy, glu_w.astype(BF16), w_out.astype(BF16))


def _layer(h, ffn1_norm, ffn1_w_gate, ffn1_w_up, ffn1_w_down, mix_norm, w_in, q_norm,
           k_norm_cmp, k_norm_slc, k_norm_win, cmp_pos_k, cmp_pos_v, cmp_k_w1, cmp_k_w2,
           cmp_v_w1, cmp_v_w2, w_nsa_proj, ssm_lambda_re, ssm_lambda_im, ssm_log_step,
           ssm_b_re, ssm_b_im, ssm_c_re, ssm_c_im, ssm_d, ssm_glu_w, w_out,
           ffn2_norm, ffn2_w_gate, ffn2_w_up, ffn2_w_down):
    b, s, _ = h.shape
    h = _ffn(h.reshape(b * s, D_MODEL), ffn1_norm, ffn1_w_gate, ffn1_w_up, ffn1_w_down).reshape(b, s, D_MODEL)
    q, kvc, ks, vs, kw, vw, gates, u = _proj(h, mix_norm, w_in[:, :P_END_SRC], q_norm, k_norm_slc, k_norm_win)
    kc, vc = _compress(kvc, cmp_pos_k, cmp_pos_v, cmp_k_w1, cmp_v_w1, cmp_k_w2, cmp_v_w2, k_norm_cmp)
    o_cmp, qs = _cmp_select(q, kc, vc, gates)
    o_slc = _slc_attn(qs, ks, vs, gates)
    o_win = _win_attn(q, kw, vw, gates)
    ar, ai, bbr, bbi = _ssm_disc(ssm_lambda_re, ssm_lambda_im, ssm_log_step, ssm_b_re, ssm_b_im)
    gy = _ssm_scan(u.transpose(1, 0, 2), ar, ai,
                   _group_diag_in(bbr).astype(BF16), _group_diag_in(bbi).astype(BF16),
                   _group_diag_out(ssm_c_re).astype(BF16), _group_diag_out(ssm_c_im).astype(BF16),
                   ssm_d.reshape(1, SSM_WIDTH))
    h = _merge(h, mix_norm, w_in[:, P_END_SRC:], o_cmp, o_slc, o_win, w_nsa_proj,
               gy.transpose(1, 0, 2).astype(BF16), ssm_glu_w, w_out)
    h = _ffn(h.reshape(b * s, D_MODEL), ffn2_norm, ffn2_w_gate, ffn2_w_up, ffn2_w_down)
    return h.reshape(b, s, D_MODEL)


P_END_SRC = NSA_WIDTH + 6 * KV_WIDTH + 3 * NSA_HEADS + SSM_WIDTH


def kernel(x, ffn1_norm, ffn1_w_gate, ffn1_w_up, ffn1_w_down, mix_norm, w_in, q_norm, k_norm_cmp, k_norm_slc, k_norm_win, cmp_pos_k, cmp_pos_v, cmp_k_w1, cmp_k_w2, cmp_v_w1, cmp_v_w2, w_nsa_proj, ssm_lambda_re, ssm_lambda_im, ssm_log_step, ssm_b_re, ssm_b_im, ssm_c_re, ssm_c_im, ssm_d, ssm_glu_w, w_out, ffn2_norm, ffn2_w_gate, ffn2_w_up, ffn2_w_down):
    params = (ffn1_norm, ffn1_w_gate, ffn1_w_up, ffn1_w_down, mix_norm, w_in, q_norm, k_norm_cmp, k_norm_slc,
              k_norm_win, cmp_pos_k, cmp_pos_v, cmp_k_w1, cmp_k_w2, cmp_v_w1, cmp_v_w2, w_nsa_proj,
              ssm_lambda_re, ssm_lambda_im, ssm_log_step, ssm_b_re, ssm_b_im, ssm_c_re, ssm_c_im, ssm_d,
              ssm_glu_w, w_out, ffn2_norm, ffn2_w_gate, ffn2_w_up, ffn2_w_down)
    h = x.astype(F32)
    for layer in range(ffn1_norm.shape[0]):
        h = _layer(h, *[p[layer] for p in params])
    return h.astype(x.dtype)
```

```python
import functools
import math

import jax
import jax.numpy as jnp
from jax import lax
from jax.experimental import pallas as pl
from jax.experimental.pallas import tpu as pltpu

F32 = jnp.float32
BF16 = jnp.bfloat16

D_MODEL = 1024
NSA_HEADS = 8
NSA_KV_GROUPS = 2
HEAD_DIM = 64
Q_PER_KV = NSA_HEADS // NSA_KV_GROUPS
NSA_WIDTH = NSA_HEADS * HEAD_DIM
KV_WIDTH = NSA_KV_GROUPS * HEAD_DIM
CMP_BLOCK = 32
CMP_STRIDE = 16
CMP_HIDDEN = 256
SLC_BLOCK = 64
SLC_TOPK = 16
WINDOW = 512
FORCE_BONUS = 1000.0
SSM_GROUP = 16
SSM_GROUPS = 32
SSM_STATE = 64
SSM_WIDTH = SSM_GROUPS * SSM_GROUP
SSM_STATES = SSM_GROUPS * SSM_STATE
D_FF = 2816
RMS_EPS = 1e-6
NEG = -1e30

LANES = 128
SUBLANES = 8
VMEM_LIMIT = 56 * 1024 * 1024

P_Q = 0
P_KVC = P_Q + NSA_HEADS * LANES
P_KS = P_KVC + 2 * KV_WIDTH
P_VS = P_KS + NSA_KV_GROUPS * LANES
P_KW = P_VS + NSA_KV_GROUPS * LANES
P_VW = P_KW + NSA_KV_GROUPS * LANES
P_GATE = P_VW + NSA_KV_GROUPS * LANES
P_U = P_GATE + NSA_KV_GROUPS * LANES
P_END = P_U + SSM_WIDTH


def _params(sem, vmem=VMEM_LIMIT):
    return pltpu.CompilerParams(dimension_semantics=sem, vmem_limit_bytes=vmem)


def _const_spec(shape):
    n = len(shape)
    return pl.BlockSpec(shape, lambda *_: (0,) * n, pipeline_mode=pl.Buffered(1))


def _rms(x, g):
    return x * lax.rsqrt(jnp.mean(x * x, axis=-1, keepdims=True) + RMS_EPS) * g


def _slot_rms(x, g):
    ms = jnp.sum(x * x, axis=-1, keepdims=True) * (1.0 / HEAD_DIM)
    return x * lax.rsqrt(ms + RMS_EPS) * g


def _dot(a, b):
    return jnp.dot(a, b, preferred_element_type=F32)


def _dot_nt(a, b):
    return lax.dot_general(a, b, (((1,), (1,)), ((), ())), preferred_element_type=F32)


MXU_DIM = 256
FF_SPLITS = (0, (D_FF // 2 + MXU_DIM - 1) // MXU_DIM * MXU_DIM, D_FF)


def _ffn_body(x_ref, g_ref, wg_ref, wu_ref, wd_ref, o_ref):
    x = x_ref[...]
    hb = _rms(x, g_ref[...]).astype(BF16)
    acc = jnp.zeros_like(x)
    for c in range(len(FF_SPLITS) - 1):
        sl = slice(FF_SPLITS[c], FF_SPLITS[c + 1])
        gate = _dot(hb, wg_ref[:, sl])
        up = _dot(hb, wu_ref[:, sl])
        act = (gate * jax.nn.sigmoid(gate) * up).astype(BF16)
        acc = acc + _dot(act, wd_ref[sl, :])
    o_ref[...] = x + 0.5 * acc


def _ffn(x2, g, wg, wu, wd, tm=512):
    t = x2.shape[0]
    return pl.pallas_call(
        _ffn_body,
        grid=(t // tm,),
        in_specs=[pl.BlockSpec((tm, D_MODEL), lambda i: (i, 0)),
                  _const_spec((1, D_MODEL)),
                  _const_spec((D_MODEL, D_FF)),
                  _const_spec((D_MODEL, D_FF)),
                  _const_spec((D_FF, D_MODEL))],
        out_specs=pl.BlockSpec((tm, D_MODEL), lambda i: (i, 0)),
        out_shape=jax.ShapeDtypeStruct((t, D_MODEL), F32),
        compiler_params=_params(("parallel",)),
        name="ffn",
    )(x2, g.reshape(1, D_MODEL), wg.astype(BF16), wu.astype(BF16), wd.astype(BF16))


def _proj_body(h_ref, g_ref, w_ref, qn_ref, ksn_ref, kwn_ref,
               q_ref, kvc_ref, ks_ref, vs_ref, kw_ref, vw_ref, gate_ref, u_ref):
    ts = h_ref.shape[1]
    hb = _rms(h_ref[0], g_ref[...]).astype(BF16)
    p = _dot(hb, w_ref[...])
    lane = lax.broadcasted_iota(jnp.int32, (ts, LANES), 1)
    tok = lax.broadcasted_iota(jnp.int32, (ts, LANES), 0) + pl.program_id(1) * ts
    low = lane < HEAD_DIM
    onehot = jnp.where(lane - HEAD_DIM == tok // SLC_BLOCK, 1.0, 0.0)
    qscale = HEAD_DIM ** -0.5 * math.log2(math.e)
    for h in range(NSA_HEADS):
        qh = p[:, P_Q + h * LANES:P_Q + (h + 1) * LANES]
        q_ref[0, h] = (_slot_rms(qh, qn_ref[...]) * qscale).astype(BF16)
    kvc_ref[0] = p[:, P_KVC:P_KS]
    for g in range(NSA_KV_GROUPS):
        sl = lambda base: slice(base + g * LANES, base + (g + 1) * LANES)
        ks = _slot_rms(p[:, sl(P_KS)], ksn_ref[...])
        ks_ref[0, g] = jnp.where(low, ks, onehot).astype(BF16)
        vs_ref[0, g] = jnp.where(low, p[:, sl(P_VS)], 1.0).astype(BF16)
        kw_ref[0, g] = _slot_rms(p[:, sl(P_KW)], kwn_ref[...]).astype(BF16)
        vw_ref[0, g] = jnp.where(low, p[:, sl(P_VW)], 1.0).astype(BF16)
        gate_ref[0, g] = jax.nn.sigmoid(p[:, sl(P_GATE)])
    u_ref[0] = p[:, P_U:P_END]


def _slot_pad(w, n):
    d = w.shape[0]
    return jnp.pad(w.reshape(d, n, HEAD_DIM), ((0, 0), (0, 0), (0, LANES - HEAD_DIM))).reshape(d, n * LANES)


def _slot_gain(g):
    return jnp.pad(g, (0, LANES - HEAD_DIM)).reshape(1, LANES)


def _proj_weight(w_in):
    cuts = [NSA_WIDTH] + [KV_WIDTH] * 6 + [3 * NSA_HEADS, SSM_WIDTH]
    offs = [0]
    for c in cuts:
        offs.append(offs[-1] + c)
    seg = lambda i: w_in[:, offs[i]:offs[i + 1]]
    gate = seg(7).reshape(D_MODEL, 3, NSA_KV_GROUPS, Q_PER_KV).transpose(0, 2, 1, 3)
    gate = gate.reshape(D_MODEL, NSA_KV_GROUPS, 3 * Q_PER_KV)
    gate = jnp.pad(gate, ((0, 0), (0, 0), (0, LANES - 3 * Q_PER_KV))).reshape(D_MODEL, NSA_KV_GROUPS * LANES)
    cols = [_slot_pad(seg(0), NSA_HEADS), seg(1), seg(2)]
    cols += [_slot_pad(seg(i), NSA_KV_GROUPS) for i in (3, 4, 5, 6)]
    cols += [gate, seg(8)]
    return jnp.concatenate(cols, axis=1).astype(BF16)


def _proj(h, mix_norm, w_in, q_norm, k_norm_slc, k_norm_win, ts=512):
    b, s, _ = h.shape
    hm = lambda n: jax.ShapeDtypeStruct((b, n, s, LANES), BF16)
    hm_spec = lambda n: pl.BlockSpec((1, n, ts, LANES), lambda bi, i: (bi, 0, i, 0))
    row_spec = lambda w: pl.BlockSpec((1, ts, w), lambda bi, i: (bi, i, 0))
    return pl.pallas_call(
        _proj_body,
        grid=(b, s // ts),
        in_specs=[row_spec(D_MODEL), _const_spec((1, D_MODEL)), _const_spec((D_MODEL, P_END)),
                  _const_spec((1, LANES)), _const_spec((1, LANES)), _const_spec((1, LANES))],
        out_specs=[hm_spec(NSA_HEADS), row_spec(2 * KV_WIDTH),
                   hm_spec(NSA_KV_GROUPS), hm_spec(NSA_KV_GROUPS), hm_spec(NSA_KV_GROUPS), hm_spec(NSA_KV_GROUPS),
                   hm_spec(NSA_KV_GROUPS), row_spec(SSM_WIDTH)],
        out_shape=[hm(NSA_HEADS), jax.ShapeDtypeStruct((b, s, 2 * KV_WIDTH), F32),
                   hm(NSA_KV_GROUPS), hm(NSA_KV_GROUPS), hm(NSA_KV_GROUPS), hm(NSA_KV_GROUPS),
                   jax.ShapeDtypeStruct((b, NSA_KV_GROUPS, s, LANES), F32),
                   jax.ShapeDtypeStruct((b, s, SSM_WIDTH), F32)],
        compiler_params=_params(("parallel", "parallel")),
        name="proj",
    )(h, mix_norm.reshape(1, D_MODEL), _proj_weight(w_in),
      _slot_gain(q_norm), _slot_gain(k_norm_slc), _slot_gain(k_norm_win))


CMP_ROW = CMP_STRIDE * 2 * KV_WIDTH
CMP_HID_ALL = 4 * CMP_HIDDEN


def _compress_body(c_ref, wlo_ref, whi_ref, pk_ref, pv_ref, w1k_ref, w1v_ref, w2_ref, kn_ref,
                   kc_ref, vc_ref):
    nch = c_ref.shape[1]
    c = c_ref[0].astype(BF16)
    first = _dot(c, wlo_ref[...])
    second = _dot(c, whi_ref[...])
    second = pltpu.roll(second, nch - 1, 0)
    rows8 = lambda r: jnp.broadcast_to(r[...].astype(BF16), (SUBLANES, r.shape[1]))
    bk = _dot(rows8(pk_ref), w1k_ref[...])[0:1]
    bv = _dot(rows8(pv_ref), w1v_ref[...])[0:1]
    bias = jnp.concatenate([bk, bk, bv, bv], axis=1)
    hid = jax.nn.gelu(first + second + bias).astype(BF16)
    out = _dot(hid, w2_ref[...])
    lane = lax.broadcasted_iota(jnp.int32, (nch, LANES), 1)
    low = lane < HEAD_DIM
    for g in range(NSA_KV_GROUPS):
        kc_ref[0, g] = _slot_rms(out[:, g * LANES:(g + 1) * LANES], kn_ref[...]).astype(BF16)
        v = out[:, (NSA_KV_GROUPS + g) * LANES:(NSA_KV_GROUPS + g + 1) * LANES]
        vc_ref[0, g] = jnp.where(low, v, 1.0).astype(BF16)


def _compress_weights(w1k, w1v, w2k, w2v):
    def half(w1, lo):
        w = w1[lo * CMP_STRIDE * HEAD_DIM:(lo + 1) * CMP_STRIDE * HEAD_DIM].reshape(CMP_STRIDE, HEAD_DIM, CMP_HIDDEN)
        return w
    eye4 = jnp.eye(4, dtype=F32)
    def build(lo):
        wk, wv = half(w1k, lo), half(w1v, lo)
        w = jnp.stack([wk, wk, wv, wv], axis=1)
        w = w[:, :, :, None, :] * eye4[None, :, None, :, None]
        return w.reshape(CMP_ROW, CMP_HID_ALL).astype(BF16)
    w2 = jnp.stack([w2k, w2k, w2v, w2v], axis=0)
    w2 = jnp.pad(w2, ((0, 0), (0, 0), (0, LANES - HEAD_DIM)))
    w2 = w2[:, :, None, :] * eye4[:, None, :, None]
    return build(0), build(1), w2.reshape(CMP_HID_ALL, 4 * LANES).astype(BF16)


def _compress(kvc, cmp_pos_k, cmp_pos_v, w1k, w1v, w2k, w2v, k_norm_cmp):
    b, s, _ = kvc.shape
    nch = s // CMP_STRIDE
    wlo, whi, w2 = _compress_weights(w1k, w1v, w2k, w2v)
    flat = CMP_BLOCK * HEAD_DIM
    out = jax.ShapeDtypeStruct((b, NSA_KV_GROUPS, nch, LANES), BF16)
    out_spec = pl.BlockSpec((1, NSA_KV_GROUPS, nch, LANES), lambda bi: (bi, 0, 0, 0))
    return pl.pallas_call(
        _compress_body,
        grid=(b,),
        in_specs=[pl.BlockSpec((1, nch, CMP_ROW), lambda bi: (bi, 0, 0)),
                  _const_spec((CMP_ROW, CMP_HID_ALL)), _const_spec((CMP_ROW, CMP_HID_ALL)),
                  _const_spec((1, flat)), _const_spec((1, flat)),
                  _const_spec((flat, CMP_HIDDEN)), _const_spec((flat, CMP_HIDDEN)),
                  _const_spec((CMP_HID_ALL, 4 * LANES)), _const_spec((1, LANES))],
        out_specs=[out_spec, out_spec],
        out_shape=[out, out],
        compiler_params=_params(("parallel",)),
        name="compress",
    )(kvc.reshape(b, nch, CMP_ROW), wlo, whi, cmp_pos_k.reshape(1, flat), cmp_pos_v.reshape(1, flat),
      w1k.astype(BF16), w1v.astype(BF16), w2, _slot_gain(k_norm_cmp))


GROUP_WIDTH = Q_PER_KV * HEAD_DIM


def _store_heads(o_ref, outs):
    o_ref[0] = jnp.concatenate([o[:, :HEAD_DIM] for o in outs], axis=1).astype(BF16)


def _attn_out(b, s, tq):
    spec = pl.BlockSpec((1, tq, GROUP_WIDTH), lambda bi, g, i: (bi, i, g))
    return spec, jax.ShapeDtypeStruct((b, s, NSA_WIDTH), BF16)


def _masked_softmax(s, mask, axis):
    s = jnp.where(mask, s, NEG)
    m = jnp.max(s, axis=axis, keepdims=True)
    e = jnp.where(mask, jnp.exp2(s - m), 0.0)
    den = jnp.sum(e, axis=axis, keepdims=True)
    return e / jnp.where(den > 0, den, 1.0)


def _topk_rows(score, k):
    n, t = score.shape
    gone = -3e38

    def peel(_, carry):
        work, cnt, thr = carry
        m = jnp.max(work, axis=0, keepdims=True)
        hit = work == m
        thr = jnp.where(cnt < k, m, thr)
        cnt = cnt + jnp.sum(jnp.where(hit, 1.0, 0.0), axis=0, keepdims=True)
        return jnp.where(hit, gone, work), cnt, thr

    init = (score, jnp.zeros((1, t), F32), jnp.full((1, t), gone, F32))
    _, _, thr = lax.fori_loop(0, k, peel, init, unroll=True)
    above = jnp.where(score > thr, 1.0, 0.0)
    tied = jnp.where(score == thr, 1.0, 0.0)
    room = k - jnp.sum(above, axis=0, keepdims=True)
    lower = lax.broadcasted_iota(jnp.int32, (n, n), 1) < lax.broadcasted_iota(jnp.int32, (n, n), 0)
    tied_before = _dot(jnp.where(lower, 1.0, 0.0).astype(BF16), tied.astype(BF16))
    return jnp.maximum(above, tied * jnp.where(tied_before < room, 1.0, 0.0))


def _cmp_select_body(q_ref, kc_ref, vc_ref, gate_ref, o_ref, qs_ref, *, ns):
    tq = q_ref.shape[2]
    nch = kc_ref.shape[2]
    t0 = pl.program_id(2) * tq
    q = q_ref[0].reshape(Q_PER_KV * tq, LANES)
    s = _dot_nt(q, kc_ref[0, 0]).reshape(Q_PER_KV, tq, nch)
    tok = t0 + lax.broadcasted_iota(jnp.int32, (tq, nch), 0)
    cid = lax.broadcasted_iota(jnp.int32, (tq, nch), 1)
    p = _masked_softmax(s, (cid * CMP_STRIDE + CMP_BLOCK - 1 <= tok)[None], axis=-1)
    o = _dot(p.reshape(Q_PER_KV * tq, nch).astype(BF16), vc_ref[0, 0]).reshape(Q_PER_KV, tq, LANES)
    gate = gate_ref[0, 0]
    _store_heads(o_ref, [o[r] * gate[:, r:r + 1] for r in range(Q_PER_KV)])
    psum = p[0] + p[1] + p[2] + p[3]
    cb = lax.broadcasted_iota(jnp.int32, (nch, LANES), 0)
    jb = lax.broadcasted_iota(jnp.int32, (nch, LANES), 1)
    overlap = jnp.where((cb * CMP_STRIDE < (jb + 1) * SLC_BLOCK) & (cb * CMP_STRIDE + CMP_BLOCK > jb * SLC_BLOCK), 1.0, 0.0)
    imp = _dot(psum, overlap).T[:ns]
    j = lax.broadcasted_iota(jnp.int32, (ns, tq), 0)
    qblk = (t0 + lax.broadcasted_iota(jnp.int32, (ns, tq), 1)) // SLC_BLOCK
    force = (j == 0) | (j == qblk) | (j == qblk - 1)
    score = jnp.where(j <= qblk, imp + FORCE_BONUS * jnp.where(force, 1.0, 0.0), NEG)
    sel_bias = jnp.where(_topk_rows(score, SLC_TOPK) > 0, 0.0, NEG)
    pieces = [jnp.zeros((HEAD_DIM, tq), F32), sel_bias]
    if ns < LANES - HEAD_DIM:
        pieces.append(jnp.zeros((LANES - HEAD_DIM - ns, tq), F32))
    bias = jnp.concatenate(pieces, axis=0).T
    for r in range(Q_PER_KV):
        qs_ref[0, r] = (q_ref[0, r].astype(F32) + bias).astype(BF16)


def _cmp_select(q, kc, vc, gates, tq=256):
    b, _, s, _ = q.shape
    nch = kc.shape[2]
    ns = s // SLC_BLOCK
    assert ns <= LANES - HEAD_DIM
    qspec = pl.BlockSpec((1, Q_PER_KV, tq, LANES), lambda bi, g, i: (bi, g, i, 0))
    cspec = pl.BlockSpec((1, 1, nch, LANES), lambda bi, g, i: (bi, g, 0, 0))
    ospec, oshape = _attn_out(b, s, tq)
    return pl.pallas_call(
        functools.partial(_cmp_select_body, ns=ns),
        grid=(b, NSA_KV_GROUPS, s // tq),
        in_specs=[qspec, cspec, cspec, pl.BlockSpec((1, 1, tq, LANES), lambda bi, g, i: (bi, g, i, 0))],
        out_specs=[ospec, qspec],
        out_shape=[oshape, jax.ShapeDtypeStruct((b, NSA_HEADS, s, LANES), BF16)],
        compiler_params=_params(("parallel", "parallel", "parallel")),
        name="cmp_select",
    )(q, kc, vc, gates)


def _slc_body(q_ref, k_ref, v_ref, gate_ref, o_ref, p_ref, m_ref, acc_ref):
    tq = q_ref.shape[2]
    i = pl.program_id(2)
    n_lane_tiles = tq // LANES

    def scores(r, j):
        k0 = pl.multiple_of(j * tq, tq)
        return _dot_nt(q_ref[0, r], k_ref[0, 0, pl.ds(k0, tq), :]).astype(BF16)

    def pending_values(r, j):
        k0 = pl.multiple_of(j * tq, tq)
        return _dot(p_ref[r], v_ref[0, 0, pl.ds(k0, tq), :])

    def row_max(s):
        part = functools.reduce(jnp.maximum, [s[:, c * LANES:(c + 1) * LANES] for c in range(n_lane_tiles)])
        return jnp.broadcast_to(jnp.max(part, axis=-1, keepdims=True), (tq, LANES))

    def probabilities(s, m):
        return jnp.concatenate([jnp.exp2(s[:, c * LANES:(c + 1) * LANES] - m) for c in range(n_lane_tiles)], axis=1)

    keep = lax.broadcasted_iota(jnp.int32, (tq, tq), 1) <= lax.broadcasted_iota(jnp.int32, (tq, tq), 0)
    for r in range(Q_PER_KV):
        s = jnp.where(keep, scores(r, i), NEG)
        m = row_max(s)
        m_ref[r] = m
        p_ref[r] = probabilities(s, m)
        acc_ref[r] = jnp.zeros((tq, LANES), F32)

    def step(j, carry):
        j_pend = jnp.where(j == 0, i, j - 1)
        for r in range(Q_PER_KV):
            pv = pending_values(r, j_pend)
            s = scores(r, j)
            m = m_ref[r]
            m_new = jnp.maximum(m, row_max(s))
            p_ref[r] = probabilities(s, m_new)
            alpha = jnp.exp2(m.astype(F32) - m_new.astype(F32))
            acc_ref[r] = alpha * (acc_ref[r] + pv)
            m_ref[r] = m_new
        return carry

    lax.fori_loop(0, i, step, 0)
    gate = gate_ref[0, 0]
    outs = []
    for r in range(Q_PER_KV):
        acc = acc_ref[r] + pending_values(r, jnp.where(i == 0, 0, i - 1))
        o = acc / acc[:, HEAD_DIM:HEAD_DIM + 1]
        outs.append(o * gate[:, Q_PER_KV + r:Q_PER_KV + r + 1])
    _store_heads(o_ref, outs)


def _slc_attn(qs, k, v, gates, tq=512):
    b, _, s, _ = qs.shape
    qspec = pl.BlockSpec((1, Q_PER_KV, tq, LANES), lambda bi, g, i: (bi, g, i, 0))
    kspec = pl.BlockSpec((1, 1, s, LANES), lambda bi, g, i: (bi, g, 0, 0))
    ospec, oshape = _attn_out(b, s, tq)
    return pl.pallas_call(
        _slc_body,
        grid=(b, NSA_KV_GROUPS, s // tq),
        in_specs=[qspec, kspec, kspec, pl.BlockSpec((1, 1, tq, LANES), lambda bi, g, i: (bi, g, i, 0))],
        out_specs=ospec,
        out_shape=oshape,
        scratch_shapes=[pltpu.VMEM((Q_PER_KV, tq, tq), BF16), pltpu.VMEM((Q_PER_KV, tq, LANES), BF16),
                        pltpu.VMEM((Q_PER_KV, tq, LANES), F32)],
        compiler_params=_params(("parallel", "parallel", "arbitrary")),
        name="slc_attn",
    )(qs, k, v, gates)


def _win_body(q_ref, k_ref, v_ref, gate_ref, o_ref):
    tq = q_ref.shape[2]
    nprev = WINDOW // tq
    i = pl.program_id(2)
    qpos = i * tq + lax.broadcasted_iota(jnp.int32, (tq, tq), 0)
    col = lax.broadcasted_iota(jnp.int32, (tq, tq), 1)
    keeps, starts = [], []
    for d in range(nprev + 1):
        jt = i - nprev + d
        diff = qpos - (jt * tq + col)
        keeps.append((diff >= 0) & (diff < WINDOW) & (jt >= 0))
        starts.append(pl.multiple_of(jnp.maximum(jt, 0) * tq, tq))
    gate = gate_ref[0, 0]
    outs = []
    for r in range(Q_PER_KV):
        q = q_ref[0, r]
        s = [jnp.where(keep, _dot_nt(q, k_ref[0, 0, pl.ds(k0, tq), :]).astype(BF16), NEG)
             for keep, k0 in zip(keeps, starts)]
        m = jnp.max(functools.reduce(jnp.maximum, s), axis=-1, keepdims=True)
        acc = sum(_dot(jnp.exp2(sd - m), v_ref[0, 0, pl.ds(k0, tq), :]) for sd, k0 in zip(s, starts))
        o = acc / acc[:, HEAD_DIM:HEAD_DIM + 1]
        outs.append(o * gate[:, 2 * Q_PER_KV + r:2 * Q_PER_KV + r + 1])
    _store_heads(o_ref, outs)


def _win_attn(q, k, v, gates, tq=512):
    b, _, s, _ = q.shape
    assert WINDOW % tq == 0
    qspec = pl.BlockSpec((1, Q_PER_KV, tq, LANES), lambda bi, g, i: (bi, g, i, 0))
    kspec = pl.BlockSpec((1, 1, s, LANES), lambda bi, g, i: (bi, g, 0, 0))
    ospec, oshape = _attn_out(b, s, tq)
    return pl.pallas_call(
        _win_body,
        grid=(b, NSA_KV_GROUPS, s // tq),
        in_specs=[qspec, kspec, kspec, pl.BlockSpec((1, 1, tq, LANES), lambda bi, g, i: (bi, g, i, 0))],
        out_specs=ospec,
        out_shape=oshape,
        compiler_params=_params(("parallel", "parallel", "arbitrary")),
        name="win_attn",
    )(q, k, v, gates)


def _ssm_disc_body(lr_ref, li_ref, ls_ref, br_ref, bi_ref, ar_ref, ai_ref, bbr_ref, bbi_ref):
    lr, li = lr_ref[...], li_ref[...]
    step = jnp.exp(ls_ref[...])
    mag = jnp.exp(lr * step)
    ar = mag * jnp.cos(li * step)
    ai = mag * jnp.sin(li * step)
    den = lr * lr + li * li
    cr = ((ar - 1.0) * lr + ai * li) / den
    ci = (ai * lr - (ar - 1.0) * li) / den
    br, bi = br_ref[...], bi_ref[...]
    ar_ref[...] = jnp.broadcast_to(ar, ar_ref.shape)
    ai_ref[...] = jnp.broadcast_to(ai, ai_ref.shape)
    bbr_ref[...] = cr * br - ci * bi
    bbi_ref[...] = cr * bi + ci * br


def _ssm_disc(lam_re, lam_im, log_step, b_re, b_im):
    row = lambda a: a.reshape(1, SSM_STATES)
    chan = lambda a: a.transpose(2, 0, 1).reshape(SSM_GROUP, SSM_STATES)
    ls = jnp.broadcast_to(log_step[:, None], (SSM_GROUPS, SSM_STATE))
    return pl.pallas_call(
        _ssm_disc_body,
        out_shape=[jax.ShapeDtypeStruct((SUBLANES, SSM_STATES), F32)] * 2
        + [jax.ShapeDtypeStruct((SSM_GROUP, SSM_STATES), F32)] * 2,
        name="ssm_disc",
    )(row(lam_re), row(lam_im), row(ls), chan(b_re), chan(b_im))


def _group_diag_in(w):
    w = w.reshape(SSM_GROUP, SSM_GROUPS, SSM_STATE)
    eye = jnp.eye(SSM_GROUPS, dtype=w.dtype)
    return (w[None] * eye[:, None, :, None]).reshape(SSM_WIDTH, SSM_STATES)


def _group_diag_out(c):
    eye = jnp.eye(SSM_GROUPS, dtype=c.dtype)
    return (c.transpose(0, 2, 1)[:, :, None, :] * eye[:, None, :, None]).reshape(SSM_STATES, SSM_WIDTH)


SCAN_COLS = 512
SSM_BLOCKS = SSM_STATES // SCAN_COLS
SCAN_CHANS = SSM_WIDTH // SSM_BLOCKS


def _diag_blocks(w):
    r, c = w.shape[0] // SSM_BLOCKS, w.shape[1] // SSM_BLOCKS
    return jnp.stack([w[m * r:(m + 1) * r, m * c:(m + 1) * c] for m in range(SSM_BLOCKS)])


def _ssm_scan_body(u_ref, ar_ref, ai_ref, wbr_ref, wbi_ref, wcr_ref, wci_ref, d_ref, o_ref,
                   xr_ref, xi_ref, vr_ref, vi_ref):
    steps = u_ref.shape[0]
    rows = steps * SUBLANES

    @pl.when(pl.program_id(0) == 0)
    def _():
        xr_ref[...] = jnp.zeros_like(xr_ref)
        xi_ref[...] = jnp.zeros_like(xi_ref)

    u = u_ref[...].reshape(rows, SSM_WIDTH)
    ub = u.astype(BF16)
    ys = []
    for cb in range(SSM_BLOCKS):
        cs = slice(cb * SCAN_COLS, (cb + 1) * SCAN_COLS)
        ubc = ub[:, cb * SCAN_CHANS:(cb + 1) * SCAN_CHANS]
        vr_ref[:, cs] = _dot(ubc, wbr_ref[cb])
        vi_ref[:, cs] = _dot(ubc, wbi_ref[cb])
        ar, ai = ar_ref[:, cs], ai_ref[:, cs]

        def step(t, carry):
            xr, xi = carry
            r0 = pl.multiple_of(t * SUBLANES, SUBLANES)
            nr = ar * xr - ai * xi + vr_ref[pl.ds(r0, SUBLANES), cs]
            ni = ar * xi + ai * xr + vi_ref[pl.ds(r0, SUBLANES), cs]
            vr_ref[pl.ds(r0, SUBLANES), cs] = nr
            vi_ref[pl.ds(r0, SUBLANES), cs] = ni
            return nr, ni

        xr, xi = lax.fori_loop(0, steps, step, (xr_ref[:, cs], xi_ref[:, cs]), unroll=8)
        xr_ref[:, cs] = xr
        xi_ref[:, cs] = xi
        ys.append(_dot(vr_ref[:, cs].astype(BF16), wcr_ref[cb]) - _dot(vi_ref[:, cs].astype(BF16), wci_ref[cb]))
    y = jnp.concatenate(ys, axis=1) + d_ref[...] * u
    o_ref[...] = jax.nn.gelu(y).reshape(steps, SUBLANES, SSM_WIDTH)


def _ssm_scan(u_tm, ar, ai, wbr, wbi, wcr, wci, d, steps=64):
    s, b, _ = u_tm.shape
    assert b == SUBLANES
    rows = steps * SUBLANES
    return pl.pallas_call(
        _ssm_scan_body,
        grid=(s // steps,),
        in_specs=[pl.BlockSpec((steps, b, SSM_WIDTH), lambda i: (i, 0, 0)),
                  _const_spec((SUBLANES, SSM_STATES)), _const_spec((SUBLANES, SSM_STATES)),
                  _const_spec((SSM_BLOCKS, SCAN_CHANS, SCAN_COLS)), _const_spec((SSM_BLOCKS, SCAN_CHANS, SCAN_COLS)),
                  _const_spec((SSM_BLOCKS, SCAN_COLS, SCAN_CHANS)), _const_spec((SSM_BLOCKS, SCAN_COLS, SCAN_CHANS)),
                  _const_spec((1, SSM_WIDTH))],
        out_specs=pl.BlockSpec((steps, b, SSM_WIDTH), lambda i: (i, 0, 0)),
        out_shape=jax.ShapeDtypeStruct((s, b, SSM_WIDTH), F32),
        scratch_shapes=[pltpu.VMEM((SUBLANES, SSM_STATES), F32), pltpu.VMEM((SUBLANES, SSM_STATES), F32),
                        pltpu.VMEM((rows, SSM_STATES), F32), pltpu.VMEM((rows, SSM_STATES), F32)],
        compiler_params=_params(("arbitrary",)),
        name="ssm_scan",
    )(u_tm, ar, ai, wbr, wbi, wcr, wci, d)


def _merge_body(h_ref, g_ref, wgate_ref, oc_ref, os_ref, ow_ref, wn_ref, gy_ref, wglu_ref, wout_ref, o_ref):
    h = h_ref[0]
    hb = _rms(h, g_ref[...]).astype(BF16)
    gates = jax.nn.sigmoid(_dot(hb, wgate_ref[...]))
    o = oc_ref[0].astype(F32) + os_ref[0].astype(F32) + ow_ref[0].astype(F32)
    y_nsa = _dot(o.astype(BF16), wn_ref[...])
    hg = _dot(gy_ref[0], wglu_ref[...])
    y_ssm = hg[:, :D_MODEL] * jax.nn.sigmoid(hg[:, D_MODEL:])
    merged = gates[:, :D_MODEL] * y_nsa + gates[:, D_MODEL:] * y_ssm
    o_ref[0] = h + _dot(merged.astype(BF16), wout_ref[...])


def _merge(h, mix_norm, w_gates, o_cmp, o_slc, o_win, w_nsa_proj, gy, glu_w, w_out, ts=512):
    b, s, _ = h.shape
    row_spec = lambda w: pl.BlockSpec((1, ts, w), lambda bi, i: (bi, i, 0))
    ospec = row_spec(NSA_WIDTH)
    return pl.pallas_call(
        _merge_body,
        grid=(b, s // ts),
        in_specs=[row_spec(D_MODEL), _const_spec((1, D_MODEL)), _const_spec((D_MODEL, 2 * D_MODEL)),
                  ospec, ospec, ospec, _const_spec((NSA_WIDTH, D_MODEL)),
                  row_spec(SSM_WIDTH), _const_spec((SSM_WIDTH, 2 * D_MODEL)), _const_spec((D_MODEL, D_MODEL))],
        out_specs=row_spec(D_MODEL),
        out_shape=jax.ShapeDtypeStruct((b, s, D_MODEL), F32),
        compiler_params=_params(("parallel", "parallel")),
        name="merge",
    )(h, mix_norm.reshape(1, D_MODEL), w_gates.astype(BF16), o_cmp, o_slc, o_win, w_nsa_proj.astype(BF16),
      gy, glu_w.astype(BF16), w_out.astype(BF16))


def _layer(h, ffn1_norm, ffn1_w_gate, ffn1_w_up, ffn1_w_down, mix_norm, w_in, q_norm,
           k_norm_cmp, k_norm_slc, k_norm_win, cmp_pos_k, cmp_pos_v, cmp_k_w1, cmp_k_w2,
           cmp_v_w1, cmp_v_w2, w_nsa_proj, ssm_lambda_re, ssm_lambda_im, ssm_log_step,
           ssm_b_re, ssm_b_im, ssm_c_re, ssm_c_im, ssm_d, ssm_glu_w, w_out,
           ffn2_norm, ffn2_w_gate, ffn2_w_up, ffn2_w_down):
    b, s, _ = h.shape
    h = _ffn(h.reshape(b * s, D_MODEL), ffn1_norm, ffn1_w_gate, ffn1_w_up, ffn1_w_down).reshape(b, s, D_MODEL)
    q, kvc, ks, vs, kw, vw, gates, u = _proj(h, mix_norm, w_in[:, :P_END_SRC], q_norm, k_norm_slc, k_norm_win)
    kc, vc = _compress(kvc, cmp_pos_k, cmp_pos_v, cmp_k_w1, cmp_v_w1, cmp_k_w2, cmp_v_w2, k_norm_cmp)
    o_cmp, qs = _cmp_select(q, kc, vc, gates)
    o_slc = _slc_attn(qs, ks, vs, gates)
    o_win = _win_attn(q, kw, vw, gates)
    ar, ai, bbr, bbi = _ssm_disc(ssm_lambda_re, ssm_lambda_im, ssm_log_step, ssm_b_re, ssm_b_im)
    gy = _ssm_scan(u.transpose(1, 0, 2), ar, ai,
                   _diag_blocks(_group_diag_in(bbr)).astype(BF16), _diag_blocks(_group_diag_in(bbi)).astype(BF16),
                   _diag_blocks(_group_diag_out(ssm_c_re)).astype(BF16),
                   _diag_blocks(_group_diag_out(ssm_c_im)).astype(BF16),
                   ssm_d.reshape(1, SSM_WIDTH))
    h = _merge(h, mix_norm, w_in[:, P_END_SRC:], o_cmp, o_slc, o_win, w_nsa_proj,
               gy.transpose(1, 0, 2).astype(BF16), ssm_glu_w, w_out)
    h = _ffn(h.reshape(b * s, D_MODEL), ffn2_norm, ffn2_w_gate, ffn2_w_up, ffn2_w_down)
    return h.reshape(b, s, D_MODEL)


P_END_SRC = NSA_WIDTH + 6 * KV_WIDTH + 3 * NSA_HEADS + SSM_WIDTH


def kernel(x, ffn1_norm, ffn1_w_gate, ffn1_w_up, ffn1_w_down, mix_norm, w_in, q_norm, k_norm_cmp, k_norm_slc, k_norm_win, cmp_pos_k, cmp_pos_v, cmp_k_w1, cmp_k_w2, cmp_v_w1, cmp_v_w2, w_nsa_proj, ssm_lambda_re, ssm_lambda_im, ssm_log_step, ssm_b_re, ssm_b_im, ssm_c_re, ssm_c_im, ssm_d, ssm_glu_w, w_out, ffn2_norm, ffn2_w_gate, ffn2_w_up, ffn2_w_down):
    params = (ffn1_norm, ffn1_w_gate, ffn1_w_up, ffn1_w_down, mix_norm, w_in, q_norm, k_norm_cmp, k_norm_slc,
              k_norm_win, cmp_pos_k, cmp_pos_v, cmp_k_w1, cmp_k_w2, cmp_v_w1, cmp_v_w2, w_nsa_proj,
              ssm_lambda_re, ssm_lambda_im, ssm_log_step, ssm_b_re, ssm_b_im, ssm_c_re, ssm_c_im, ssm_d,
              ssm_glu_w, w_out, ffn2_norm, ffn2_w_gate, ffn2_w_up, ffn2_w_down)
    h = x.astype(F32)
    for layer in range(ffn1_norm.shape[0]):
        h = _layer(h, *[p[layer] for p in params])
    return h.astype(x.dtype)
```

```python
import functools
import math

import jax
import jax.numpy as jnp
from jax import lax
from jax.experimental import pallas as pl
from jax.experimental.pallas import tpu as pltpu

F32 = jnp.float32
BF16 = jnp.bfloat16

D_MODEL = 1024
NSA_HEADS = 8
NSA_KV_GROUPS = 2
HEAD_DIM = 64
Q_PER_KV = NSA_HEADS // NSA_KV_GROUPS
NSA_WIDTH = NSA_HEADS * HEAD_DIM
KV_WIDTH = NSA_KV_GROUPS * HEAD_DIM
CMP_BLOCK = 32
CMP_STRIDE = 16
CMP_HIDDEN = 256
SLC_BLOCK = 64
SLC_TOPK = 16
WINDOW = 512
FORCE_BONUS = 1000.0
SSM_GROUP = 16
SSM_GROUPS = 32
SSM_STATE = 64
SSM_WIDTH = SSM_GROUPS * SSM_GROUP
SSM_STATES = SSM_GROUPS * SSM_STATE
D_FF = 2816
RMS_EPS = 1e-6
NEG = -1e30

LANES = 128
SUBLANES = 8
VMEM_LIMIT = 56 * 1024 * 1024

P_Q = 0
P_KVC = P_Q + NSA_WIDTH
P_KS = P_KVC + 2 * KV_WIDTH
P_VS = P_KS + KV_WIDTH
P_KW = P_VS + KV_WIDTH
P_VW = P_KW + KV_WIDTH
P_GATE = P_VW + KV_WIDTH
P_U = P_GATE + NSA_KV_GROUPS * LANES
P_END = P_U + SSM_WIDTH


def _params(sem, vmem=VMEM_LIMIT):
    return pltpu.CompilerParams(dimension_semantics=sem, vmem_limit_bytes=vmem)


def _const_spec(shape):
    n = len(shape)
    return pl.BlockSpec(shape, lambda *_: (0,) * n, pipeline_mode=pl.Buffered(1))


def _rms(x, g):
    return x * lax.rsqrt(jnp.mean(x * x, axis=-1, keepdims=True) + RMS_EPS) * g


def _slot_rms(x, g):
    ms = jnp.sum(x * x, axis=-1, keepdims=True) * (1.0 / HEAD_DIM)
    return x * lax.rsqrt(ms + RMS_EPS) * g


def _dot(a, b):
    return jnp.dot(a, b, preferred_element_type=F32)


def _dot_nt(a, b):
    return lax.dot_general(a, b, (((1,), (1,)), ((), ())), preferred_element_type=F32)


MXU_DIM = 256
FF_SPLITS = (0, (D_FF // 2 + MXU_DIM - 1) // MXU_DIM * MXU_DIM, D_FF)


def _ffn_body(x_ref, g_ref, wg_ref, wu_ref, wd_ref, o_ref):
    x = x_ref[...]
    hb = _rms(x, g_ref[...]).astype(BF16)
    acc = jnp.zeros_like(x)
    for c in range(len(FF_SPLITS) - 1):
        sl = slice(FF_SPLITS[c], FF_SPLITS[c + 1])
        gate = _dot(hb, wg_ref[:, sl])
        up = _dot(hb, wu_ref[:, sl])
        act = (gate * jax.nn.sigmoid(gate) * up).astype(BF16)
        acc = acc + _dot(act, wd_ref[sl, :])
    o_ref[...] = x + 0.5 * acc


def _ffn(x2, g, wg, wu, wd, tm=512):
    t = x2.shape[0]
    return pl.pallas_call(
        _ffn_body,
        grid=(t // tm,),
        in_specs=[pl.BlockSpec((tm, D_MODEL), lambda i: (i, 0)),
                  _const_spec((1, D_MODEL)),
                  _const_spec((D_MODEL, D_FF)),
                  _const_spec((D_MODEL, D_FF)),
                  _const_spec((D_FF, D_MODEL))],
        out_specs=pl.BlockSpec((tm, D_MODEL), lambda i: (i, 0)),
        out_shape=jax.ShapeDtypeStruct((t, D_MODEL), F32),
        compiler_params=_params(("parallel",)),
        name="ffn",
    )(x2, g.reshape(1, D_MODEL), wg.astype(BF16), wu.astype(BF16), wd.astype(BF16))


def _pair_rms(x, gain, low):
    sq = x * x
    ms_low = jnp.sum(jnp.where(low, sq, 0.0), axis=-1, keepdims=True)
    ms_high = jnp.sum(jnp.where(low, 0.0, sq), axis=-1, keepdims=True)
    ms = jnp.where(low, ms_low, ms_high) * (1.0 / HEAD_DIM)
    return x * lax.rsqrt(ms + RMS_EPS) * gain


def _pair_slots(x, low, fill):
    return jnp.where(low, x, fill), jnp.where(low, pltpu.roll(x, HEAD_DIM, 1), fill)


def _proj_body(h_ref, g_ref, w_ref, qn_ref, ksn_ref, kwn_ref,
               q_ref, kvc_ref, ks_ref, vs_ref, kw_ref, vw_ref, gate_ref, u_ref):
    ts = h_ref.shape[1]
    hb = _rms(h_ref[0], g_ref[...]).astype(BF16)
    p = _dot(hb, w_ref[...])
    lane = lax.broadcasted_iota(jnp.int32, (ts, LANES), 1)
    tok = lax.broadcasted_iota(jnp.int32, (ts, LANES), 0) + pl.program_id(1) * ts
    low = lane < HEAD_DIM
    onehot = jnp.where(lane - HEAD_DIM == tok // SLC_BLOCK, 1.0, 0.0)
    qscale = HEAD_DIM ** -0.5 * math.log2(math.e)
    tile = lambda base, c=0: p[:, base + c * LANES:base + (c + 1) * LANES]
    for c in range(NSA_HEADS // 2):
        pair = _pair_slots(_pair_rms(tile(P_Q, c), qn_ref[...], low) * qscale, low, 0.0)
        q_ref[0, 2 * c] = pair[0].astype(BF16)
        q_ref[0, 2 * c + 1] = pair[1].astype(BF16)
    kvc_ref[0] = p[:, P_KVC:P_KS]
    outs = ((ks_ref, _pair_slots(_pair_rms(tile(P_KS), ksn_ref[...], low), low, onehot)),
            (vs_ref, _pair_slots(tile(P_VS), low, 1.0)),
            (kw_ref, _pair_slots(_pair_rms(tile(P_KW), kwn_ref[...], low), low, 0.0)),
            (vw_ref, _pair_slots(tile(P_VW), low, 1.0)))
    for ref, pair in outs:
        for g in range(NSA_KV_GROUPS):
            ref[0, g] = pair[g].astype(BF16)
    for g in range(NSA_KV_GROUPS):
        gate_ref[0, g] = jax.nn.sigmoid(tile(P_GATE, g))
    u_ref[0] = p[:, P_U:P_END].astype(BF16)


def _slot_gain(g):
    return jnp.pad(g, (0, LANES - HEAD_DIM)).reshape(1, LANES)


def _pair_gain(g):
    return jnp.tile(g, LANES // HEAD_DIM).reshape(1, LANES)


def _proj_weight(w_in):
    n_head_cols = NSA_WIDTH + 6 * KV_WIDTH
    n_gate = 3 * NSA_HEADS
    gate = w_in[:, n_head_cols:n_head_cols + n_gate]
    gate = gate.reshape(D_MODEL, 3, NSA_KV_GROUPS, Q_PER_KV).transpose(0, 2, 1, 3)
    gate = gate.reshape(D_MODEL, NSA_KV_GROUPS, 3 * Q_PER_KV)
    gate = jnp.pad(gate, ((0, 0), (0, 0), (0, LANES - 3 * Q_PER_KV))).reshape(D_MODEL, NSA_KV_GROUPS * LANES)
    cols = [w_in[:, :n_head_cols], gate, w_in[:, n_head_cols + n_gate:]]
    return jnp.concatenate(cols, axis=1).astype(BF16)


def _proj(h, mix_norm, w_in, q_norm, k_norm_slc, k_norm_win, ts=512):
    b, s, _ = h.shape
    hm = lambda n: jax.ShapeDtypeStruct((b, n, s, LANES), BF16)
    hm_spec = lambda n: pl.BlockSpec((1, n, ts, LANES), lambda bi, i: (bi, 0, i, 0))
    row_spec = lambda w: pl.BlockSpec((1, ts, w), lambda bi, i: (bi, i, 0))
    return pl.pallas_call(
        _proj_body,
        grid=(b, s // ts),
        in_specs=[row_spec(D_MODEL), _const_spec((1, D_MODEL)), _const_spec((D_MODEL, P_END)),
                  _const_spec((1, LANES)), _const_spec((1, LANES)), _const_spec((1, LANES))],
        out_specs=[hm_spec(NSA_HEADS), row_spec(2 * KV_WIDTH),
                   hm_spec(NSA_KV_GROUPS), hm_spec(NSA_KV_GROUPS), hm_spec(NSA_KV_GROUPS), hm_spec(NSA_KV_GROUPS),
                   hm_spec(NSA_KV_GROUPS), row_spec(SSM_WIDTH)],
        out_shape=[hm(NSA_HEADS), jax.ShapeDtypeStruct((b, s, 2 * KV_WIDTH), F32),
                   hm(NSA_KV_GROUPS), hm(NSA_KV_GROUPS), hm(NSA_KV_GROUPS), hm(NSA_KV_GROUPS),
                   jax.ShapeDtypeStruct((b, NSA_KV_GROUPS, s, LANES), F32),
                   jax.ShapeDtypeStruct((b, s, SSM_WIDTH), BF16)],
        compiler_params=_params(("parallel", "parallel")),
        name="proj",
    )(h, mix_norm.reshape(1, D_MODEL), _proj_weight(w_in),
      _pair_gain(q_norm), _pair_gain(k_norm_slc), _pair_gain(k_norm_win))


CMP_ROW = CMP_STRIDE * 2 * KV_WIDTH
CMP_HID_ALL = 4 * CMP_HIDDEN


def _compress_body(c_ref, wlo_ref, whi_ref, pk_ref, pv_ref, w1k_ref, w1v_ref, w2_ref, kn_ref,
                   kc_ref, vc_ref):
    nch = c_ref.shape[1]
    c = c_ref[0].astype(BF16)
    first = _dot(c, wlo_ref[...])
    second = _dot(c, whi_ref[...])
    second = pltpu.roll(second, nch - 1, 0)
    rows8 = lambda r: jnp.broadcast_to(r[...].astype(BF16), (SUBLANES, r.shape[1]))
    bk = _dot(rows8(pk_ref), w1k_ref[...])[0:1]
    bv = _dot(rows8(pv_ref), w1v_ref[...])[0:1]
    bias = jnp.concatenate([bk, bk, bv, bv], axis=1)
    hid = jax.nn.gelu(first + second + bias).astype(BF16)
    out = _dot(hid, w2_ref[...])
    lane = lax.broadcasted_iota(jnp.int32, (nch, LANES), 1)
    low = lane < HEAD_DIM
    for g in range(NSA_KV_GROUPS):
        kc_ref[0, g] = _slot_rms(out[:, g * LANES:(g + 1) * LANES], kn_ref[...]).astype(BF16)
        v = out[:, (NSA_KV_GROUPS + g) * LANES:(NSA_KV_GROUPS + g + 1) * LANES]
        vc_ref[0, g] = jnp.where(low, v, 1.0).astype(BF16)


def _compress_weights(w1k, w1v, w2k, w2v):
    def half(w1, lo):
        w = w1[lo * CMP_STRIDE * HEAD_DIM:(lo + 1) * CMP_STRIDE * HEAD_DIM].reshape(CMP_STRIDE, HEAD_DIM, CMP_HIDDEN)
        return w
    eye4 = jnp.eye(4, dtype=F32)
    def build(lo):
        wk, wv = half(w1k, lo), half(w1v, lo)
        w = jnp.stack([wk, wk, wv, wv], axis=1)
        w = w[:, :, :, None, :] * eye4[None, :, None, :, None]
        return w.reshape(CMP_ROW, CMP_HID_ALL).astype(BF16)
    w2 = jnp.stack([w2k, w2k, w2v, w2v], axis=0)
    w2 = jnp.pad(w2, ((0, 0), (0, 0), (0, LANES - HEAD_DIM)))
    w2 = w2[:, :, None, :] * eye4[:, None, :, None]
    return build(0), build(1), w2.reshape(CMP_HID_ALL, 4 * LANES).astype(BF16)


def _compress(kvc, cmp_pos_k, cmp_pos_v, w1k, w1v, w2k, w2v, k_norm_cmp):
    b, s, _ = kvc.shape
    nch = s // CMP_STRIDE
    wlo, whi, w2 = _compress_weights(w1k, w1v, w2k, w2v)
    flat = CMP_BLOCK * HEAD_DIM
    out = jax.ShapeDtypeStruct((b, NSA_KV_GROUPS, nch, LANES), BF16)
    out_spec = pl.BlockSpec((1, NSA_KV_GROUPS, nch, LANES), lambda bi: (bi, 0, 0, 0))
    return pl.pallas_call(
        _compress_body,
        grid=(b,),
        in_specs=[pl.BlockSpec((1, nch, CMP_ROW), lambda bi: (bi, 0, 0)),
                  _const_spec((CMP_ROW, CMP_HID_ALL)), _const_spec((CMP_ROW, CMP_HID_ALL)),
                  _const_spec((1, flat)), _const_spec((1, flat)),
                  _const_spec((flat, CMP_HIDDEN)), _const_spec((flat, CMP_HIDDEN)),
                  _const_spec((CMP_HID_ALL, 4 * LANES)), _const_spec((1, LANES))],
        out_specs=[out_spec, out_spec],
        out_shape=[out, out],
        compiler_params=_params(("parallel",)),
        name="compress",
    )(kvc.reshape(b, nch, CMP_ROW), wlo, whi, cmp_pos_k.reshape(1, flat), cmp_pos_v.reshape(1, flat),
      w1k.astype(BF16), w1v.astype(BF16), w2, _slot_gain(k_norm_cmp))


GROUP_WIDTH = Q_PER_KV * HEAD_DIM


def _store_heads(o_ref, outs):
    o_ref[0] = jnp.concatenate([o[:, :HEAD_DIM] for o in outs], axis=1).astype(BF16)


def _attn_out(b, s, tq):
    spec = pl.BlockSpec((1, tq, GROUP_WIDTH), lambda bi, g, i: (bi, i, g))
    return spec, jax.ShapeDtypeStruct((b, s, NSA_WIDTH), BF16)


def _masked_softmax(s, mask, axis):
    s = jnp.where(mask, s, NEG)
    m = jnp.max(s, axis=axis, keepdims=True)
    e = jnp.where(mask, jnp.exp2(s - m), 0.0)
    den = jnp.sum(e, axis=axis, keepdims=True)
    return e / jnp.where(den > 0, den, 1.0)


def _topk_rows(score, k):
    n, t = score.shape
    gone = -3e38

    def peel(_, carry):
        work, cnt, thr = carry
        m = jnp.max(work, axis=0, keepdims=True)
        hit = work == m
        thr = jnp.where(cnt < k, m, thr)
        cnt = cnt + jnp.sum(jnp.where(hit, 1.0, 0.0), axis=0, keepdims=True)
        return jnp.where(hit, gone, work), cnt, thr

    init = (score, jnp.zeros((1, t), F32), jnp.full((1, t), gone, F32))
    _, _, thr = lax.fori_loop(0, k, peel, init, unroll=True)
    above = jnp.where(score > thr, 1.0, 0.0)
    tied = jnp.where(score == thr, 1.0, 0.0)
    room = k - jnp.sum(above, axis=0, keepdims=True)
    lower = lax.broadcasted_iota(jnp.int32, (n, n), 1) < lax.broadcasted_iota(jnp.int32, (n, n), 0)
    tied_before = _dot(jnp.where(lower, 1.0, 0.0).astype(BF16), tied.astype(BF16))
    return jnp.maximum(above, tied * jnp.where(tied_before < room, 1.0, 0.0))


def _cmp_select_body(q_ref, kc_ref, vc_ref, gate_ref, o_ref, qs_ref, *, ns):
    tq = q_ref.shape[2]
    nch = kc_ref.shape[2]
    t0 = pl.program_id(2) * tq
    q = q_ref[0].reshape(Q_PER_KV * tq, LANES)
    s = _dot_nt(q, kc_ref[0, 0]).reshape(Q_PER_KV, tq, nch)
    tok = t0 + lax.broadcasted_iota(jnp.int32, (tq, nch), 0)
    cid = lax.broadcasted_iota(jnp.int32, (tq, nch), 1)
    p = _masked_softmax(s, (cid * CMP_STRIDE + CMP_BLOCK - 1 <= tok)[None], axis=-1)
    o = _dot(p.reshape(Q_PER_KV * tq, nch).astype(BF16), vc_ref[0, 0]).reshape(Q_PER_KV, tq, LANES)
    gate = gate_ref[0, 0]
    _store_heads(o_ref, [o[r] * gate[:, r:r + 1] for r in range(Q_PER_KV)])
    psum = p[0] + p[1] + p[2] + p[3]
    cb = lax.broadcasted_iota(jnp.int32, (nch, LANES), 0)
    jb = lax.broadcasted_iota(jnp.int32, (nch, LANES), 1)
    overlap = jnp.where((cb * CMP_STRIDE < (jb + 1) * SLC_BLOCK) & (cb * CMP_STRIDE + CMP_BLOCK > jb * SLC_BLOCK), 1.0, 0.0)
    imp = _dot(psum, overlap).T[:ns]
    j = lax.broadcasted_iota(jnp.int32, (ns, tq), 0)
    qblk = (t0 + lax.broadcasted_iota(jnp.int32, (ns, tq), 1)) // SLC_BLOCK
    force = (j == 0) | (j == qblk) | (j == qblk - 1)
    score = jnp.where(j <= qblk, imp + FORCE_BONUS * jnp.where(force, 1.0, 0.0), NEG)
    sel_bias = jnp.where(_topk_rows(score, SLC_TOPK) > 0, 0.0, NEG)
    pieces = [jnp.zeros((HEAD_DIM, tq), F32), sel_bias]
    if ns < LANES - HEAD_DIM:
        pieces.append(jnp.zeros((LANES - HEAD_DIM - ns, tq), F32))
    bias = jnp.concatenate(pieces, axis=0).T
    for r in range(Q_PER_KV):
        qs_ref[0, r] = (q_ref[0, r].astype(F32) + bias).astype(BF16)


def _cmp_select(q, kc, vc, gates, tq=256):
    b, _, s, _ = q.shape
    nch = kc.shape[2]
    ns = s // SLC_BLOCK
    assert ns <= LANES - HEAD_DIM
    qspec = pl.BlockSpec((1, Q_PER_KV, tq, LANES), lambda bi, g, i: (bi, g, i, 0))
    cspec = pl.BlockSpec((1, 1, nch, LANES), lambda bi, g, i: (bi, g, 0, 0))
    ospec, oshape = _attn_out(b, s, tq)
    return pl.pallas_call(
        functools.partial(_cmp_select_body, ns=ns),
        grid=(b, NSA_KV_GROUPS, s // tq),
        in_specs=[qspec, cspec, cspec, pl.BlockSpec((1, 1, tq, LANES), lambda bi, g, i: (bi, g, i, 0))],
        out_specs=[ospec, qspec],
        out_shape=[oshape, jax.ShapeDtypeStruct((b, NSA_HEADS, s, LANES), BF16)],
        compiler_params=_params(("parallel", "parallel", "parallel")),
        name="cmp_select",
    )(q, kc, vc, gates)


def _slc_body(q_ref, k_ref, v_ref, gate_ref, o_ref, p_ref, m_ref, acc_ref):
    tq = q_ref.shape[2]
    i = pl.program_id(2)
    n_lane_tiles = tq // LANES

    def scores(r, j):
        k0 = pl.multiple_of(j * tq, tq)
        return _dot_nt(q_ref[0, r], k_ref[0, 0, pl.ds(k0, tq), :]).astype(BF16)

    def pending_values(r, j):
        k0 = pl.multiple_of(j * tq, tq)
        return _dot(p_ref[r], v_ref[0, 0, pl.ds(k0, tq), :])

    def row_max(s):
        part = functools.reduce(jnp.maximum, [s[:, c * LANES:(c + 1) * LANES] for c in range(n_lane_tiles)])
        return jnp.broadcast_to(jnp.max(part, axis=-1, keepdims=True), (tq, LANES))

    def probabilities(s, m):
        return jnp.concatenate([jnp.exp2(s[:, c * LANES:(c + 1) * LANES] - m) for c in range(n_lane_tiles)], axis=1)

    keep = lax.broadcasted_iota(jnp.int32, (tq, tq), 1) <= lax.broadcasted_iota(jnp.int32, (tq, tq), 0)
    for r in range(Q_PER_KV):
        s = jnp.where(keep, scores(r, i), NEG)
        m = row_max(s)
        m_ref[r] = m
        p_ref[r] = probabilities(s, m)
        acc_ref[r] = jnp.zeros((tq, LANES), F32)

    def step(j, carry):
        j_pend = jnp.where(j == 0, i, j - 1)
        for r in range(Q_PER_KV):
            pv = pending_values(r, j_pend)
            s = scores(r, j)
            m = m_ref[r]
            m_new = jnp.maximum(m, row_max(s))
            p_ref[r] = probabilities(s, m_new)
            alpha = jnp.exp2(m.astype(F32) - m_new.astype(F32))
            acc_ref[r] = alpha * (acc_ref[r] + pv)
            m_ref[r] = m_new
        return carry

    lax.fori_loop(0, i, step, 0)
    gate = gate_ref[0, 0]
    outs = []
    for r in range(Q_PER_KV):
        acc = acc_ref[r] + pending_values(r, jnp.where(i == 0, 0, i - 1))
        o = acc / acc[:, HEAD_DIM:HEAD_DIM + 1]
        outs.append(o * gate[:, Q_PER_KV + r:Q_PER_KV + r + 1])
    _store_heads(o_ref, outs)


def _slc_attn(qs, k, v, gates, tq=512):
    b, _, s, _ = qs.shape
    qspec = pl.BlockSpec((1, Q_PER_KV, tq, LANES), lambda bi, g, i: (bi, g, i, 0))
    kspec = pl.BlockSpec((1, 1, s, LANES), lambda bi, g, i: (bi, g, 0, 0))
    ospec, oshape = _attn_out(b, s, tq)
    return pl.pallas_call(
        _slc_body,
        grid=(b, NSA_KV_GROUPS, s // tq),
        in_specs=[qspec, kspec, kspec, pl.BlockSpec((1, 1, tq, LANES), lambda bi, g, i: (bi, g, i, 0))],
        out_specs=ospec,
        out_shape=oshape,
        scratch_shapes=[pltpu.VMEM((Q_PER_KV, tq, tq), BF16), pltpu.VMEM((Q_PER_KV, tq, LANES), BF16),
                        pltpu.VMEM((Q_PER_KV, tq, LANES), F32)],
        compiler_params=_params(("parallel", "parallel", "arbitrary")),
        name="slc_attn",
    )(qs, k, v, gates)


def _win_body(q_ref, k_ref, v_ref, gate_ref, o_ref):
    tq = q_ref.shape[2]
    nprev = WINDOW // tq
    i = pl.program_id(2)
    qpos = i * tq + lax.broadcasted_iota(jnp.int32, (tq, tq), 0)
    col = lax.broadcasted_iota(jnp.int32, (tq, tq), 1)
    keeps, starts = [], []
    for d in range(nprev + 1):
        jt = i - nprev + d
        diff = qpos - (jt * tq + col)
        keeps.append((diff >= 0) & (diff < WINDOW) & (jt >= 0))
        starts.append(pl.multiple_of(jnp.maximum(jt, 0) * tq, tq))
    gate = gate_ref[0, 0]

    outs = []
    for r in range(Q_PER_KV):
        q = q_ref[0, r]
        s = [jnp.where(keep, _dot_nt(q, k_ref[0, 0, pl.ds(k0, tq), :]).astype(BF16), NEG)
             for keep, k0 in zip(keeps, starts)]
        m = jnp.max(functools.reduce(jnp.maximum, s), axis=-1, keepdims=True)
        acc = sum(_dot(jnp.exp2(sd - m), v_ref[0, 0, pl.ds(k0, tq), :]) for sd, k0 in zip(s, starts))
        o = acc / acc[:, HEAD_DIM:HEAD_DIM + 1]
        outs.append(o * gate[:, 2 * Q_PER_KV + r:2 * Q_PER_KV + r + 1])
    _store_heads(o_ref, outs)


def _win_attn(q, k, v, gates, tq=512):
    b, _, s, _ = q.shape
    assert WINDOW % tq == 0
    qspec = pl.BlockSpec((1, Q_PER_KV, tq, LANES), lambda bi, g, i: (bi, g, i, 0))
    kspec = pl.BlockSpec((1, 1, s, LANES), lambda bi, g, i: (bi, g, 0, 0))
    ospec, oshape = _attn_out(b, s, tq)
    return pl.pallas_call(
        _win_body,
        grid=(b, NSA_KV_GROUPS, s // tq),
        in_specs=[qspec, kspec, kspec, pl.BlockSpec((1, 1, tq, LANES), lambda bi, g, i: (bi, g, i, 0))],
        out_specs=ospec,
        out_shape=oshape,
        compiler_params=_params(("parallel", "parallel", "arbitrary")),
        name="win_attn",
    )(q, k, v, gates)


def _ssm_disc_body(lr_ref, li_ref, ls_ref, br_ref, bi_ref, ar_ref, ai_ref, bbr_ref, bbi_ref):
    lr, li = lr_ref[...], li_ref[...]
    step = jnp.exp(ls_ref[...])
    mag = jnp.exp(lr * step)
    ar = mag * jnp.cos(li * step)
    ai = mag * jnp.sin(li * step)
    den = lr * lr + li * li
    cr = ((ar - 1.0) * lr + ai * li) / den
    ci = (ai * lr - (ar - 1.0) * li) / den
    br, bi = br_ref[...], bi_ref[...]
    ar_ref[...] = jnp.broadcast_to(ar, ar_ref.shape)
    ai_ref[...] = jnp.broadcast_to(ai, ai_ref.shape)
    bbr_ref[...] = cr * br - ci * bi
    bbi_ref[...] = cr * bi + ci * br


def _ssm_disc(lam_re, lam_im, log_step, b_re, b_im):
    row = lambda a: a.reshape(1, SSM_STATES)
    chan = lambda a: a.transpose(2, 0, 1).reshape(SSM_GROUP, SSM_STATES)
    ls = jnp.broadcast_to(log_step[:, None], (SSM_GROUPS, SSM_STATE))
    return pl.pallas_call(
        _ssm_disc_body,
        out_shape=[jax.ShapeDtypeStruct((SUBLANES, SSM_STATES), F32)] * 2
        + [jax.ShapeDtypeStruct((SSM_GROUP, SSM_STATES), F32)] * 2,
        name="ssm_disc",
    )(row(lam_re), row(lam_im), row(ls), chan(b_re), chan(b_im))


def _group_diag_in(w):
    w = w.reshape(SSM_GROUP, SSM_GROUPS, SSM_STATE)
    eye = jnp.eye(SSM_GROUPS, dtype=w.dtype)
    return (w[None] * eye[:, None, :, None]).reshape(SSM_WIDTH, SSM_STATES)


def _group_diag_out(c):
    eye = jnp.eye(SSM_GROUPS, dtype=c.dtype)
    return (c.transpose(0, 2, 1)[:, :, None, :] * eye[:, None, :, None]).reshape(SSM_STATES, SSM_WIDTH)


SCAN_COLS = 512
SSM_BLOCKS = SSM_STATES // SCAN_COLS
SCAN_CHANS = SSM_WIDTH // SSM_BLOCKS


def _diag_blocks(w):
    r, c = w.shape[0] // SSM_BLOCKS, w.shape[1] // SSM_BLOCKS
    return jnp.stack([w[m * r:(m + 1) * r, m * c:(m + 1) * c] for m in range(SSM_BLOCKS)])


def _ssm_scan_body(u_ref, ar_ref, ai_ref, wbr_ref, wbi_ref, wcr_ref, wci_ref, d_ref, o_ref,
                   xr_ref, xi_ref, vr_ref, vi_ref):
    rows = u_ref.shape[0]
    steps = rows // SUBLANES

    @pl.when(pl.program_id(0) == 0)
    def _():
        xr_ref[...] = jnp.zeros_like(xr_ref)
        xi_ref[...] = jnp.zeros_like(xi_ref)

    ub = u_ref[...]
    u = ub.astype(F32)
    ys = []
    for cb in range(SSM_BLOCKS):
        cs = slice(cb * SCAN_COLS, (cb + 1) * SCAN_COLS)
        ubc = ub[:, cb * SCAN_CHANS:(cb + 1) * SCAN_CHANS]
        vr_ref[:, cs] = _dot(ubc, wbr_ref[cb])
        vi_ref[:, cs] = _dot(ubc, wbi_ref[cb])
        ar, ai = ar_ref[:, cs], ai_ref[:, cs]

        def step(t, carry):
            xr, xi = carry
            r0 = pl.multiple_of(t * SUBLANES, SUBLANES)
            nr = ar * xr - ai * xi + vr_ref[pl.ds(r0, SUBLANES), cs]
            ni = ar * xi + ai * xr + vi_ref[pl.ds(r0, SUBLANES), cs]
            vr_ref[pl.ds(r0, SUBLANES), cs] = nr
            vi_ref[pl.ds(r0, SUBLANES), cs] = ni
            return nr, ni

        xr, xi = lax.fori_loop(0, steps, step, (xr_ref[:, cs], xi_ref[:, cs]), unroll=8)
        xr_ref[:, cs] = xr
        xi_ref[:, cs] = xi
        ys.append(_dot(vr_ref[:, cs].astype(BF16), wcr_ref[cb]) - _dot(vi_ref[:, cs].astype(BF16), wci_ref[cb]))
    y = jnp.concatenate(ys, axis=1) + d_ref[...] * u
    o_ref[...] = jax.nn.gelu(y).astype(BF16)


def _ssm_scan(u_tm, ar, ai, wbr, wbi, wcr, wci, d, steps=64):
    rows = steps * SUBLANES
    n = u_tm.shape[0]
    return pl.pallas_call(
        _ssm_scan_body,
        grid=(n // rows,),
        in_specs=[pl.BlockSpec((rows, SSM_WIDTH), lambda i: (i, 0)),
                  _const_spec((SUBLANES, SSM_STATES)), _const_spec((SUBLANES, SSM_STATES)),
                  _const_spec((SSM_BLOCKS, SCAN_CHANS, SCAN_COLS)), _const_spec((SSM_BLOCKS, SCAN_CHANS, SCAN_COLS)),
                  _const_spec((SSM_BLOCKS, SCAN_COLS, SCAN_CHANS)), _const_spec((SSM_BLOCKS, SCAN_COLS, SCAN_CHANS)),
                  _const_spec((1, SSM_WIDTH))],
        out_specs=pl.BlockSpec((rows, SSM_WIDTH), lambda i: (i, 0)),
        out_shape=jax.ShapeDtypeStruct((n, SSM_WIDTH), BF16),
        scratch_shapes=[pltpu.VMEM((SUBLANES, SSM_STATES), F32), pltpu.VMEM((SUBLANES, SSM_STATES), F32),
                        pltpu.VMEM((rows, SSM_STATES), F32), pltpu.VMEM((rows, SSM_STATES), F32)],
        compiler_params=_params(("arbitrary",)),
        name="ssm_scan",
    )(u_tm, ar, ai, wbr, wbi, wcr, wci, d)


def _merge_body(h_ref, g_ref, wgate_ref, oc_ref, os_ref, ow_ref, wn_ref, gy_ref, wglu_ref, wout_ref, o_ref):
    h = h_ref[0]
    hb = _rms(h, g_ref[...]).astype(BF16)
    gates = jax.nn.sigmoid(_dot(hb, wgate_ref[...]))
    o = oc_ref[0].astype(F32) + os_ref[0].astype(F32) + ow_ref[0].astype(F32)
    y_nsa = _dot(o.astype(BF16), wn_ref[...])
    hg = _dot(gy_ref[0], wglu_ref[...])
    y_ssm = hg[:, :D_MODEL] * jax.nn.sigmoid(hg[:, D_MODEL:])
    merged = gates[:, :D_MODEL] * y_nsa + gates[:, D_MODEL:] * y_ssm
    o_ref[0] = h + _dot(merged.astype(BF16), wout_ref[...])


def _merge(h, mix_norm, w_gates, o_cmp, o_slc, o_win, w_nsa_proj, gy, glu_w, w_out, ts=512):
    b, s, _ = h.shape
    row_spec = lambda w: pl.BlockSpec((1, ts, w), lambda bi, i: (bi, i, 0))
    ospec = row_spec(NSA_WIDTH)
    return pl.pallas_call(
        _merge_body,
        grid=(b, s // ts),
        in_specs=[row_spec(D_MODEL), _const_spec((1, D_MODEL)), _const_spec((D_MODEL, 2 * D_MODEL)),
                  ospec, ospec, ospec, _const_spec((NSA_WIDTH, D_MODEL)),
                  row_spec(SSM_WIDTH), _const_spec((SSM_WIDTH, 2 * D_MODEL)), _const_spec((D_MODEL, D_MODEL))],
        out_specs=row_spec(D_MODEL),
        out_shape=jax.ShapeDtypeStruct((b, s, D_MODEL), F32),
        compiler_params=_params(("parallel", "parallel")),
        name="merge",
    )(h, mix_norm.reshape(1, D_MODEL), w_gates.astype(BF16), o_cmp, o_slc, o_win, w_nsa_proj.astype(BF16),
      gy, glu_w.astype(BF16), w_out.astype(BF16))


def _layer(h, ffn1_norm, ffn1_w_gate, ffn1_w_up, ffn1_w_down, mix_norm, w_in, q_norm,
           k_norm_cmp, k_norm_slc, k_norm_win, cmp_pos_k, cmp_pos_v, cmp_k_w1, cmp_k_w2,
           cmp_v_w1, cmp_v_w2, w_nsa_proj, ssm_lambda_re, ssm_lambda_im, ssm_log_step,
           ssm_b_re, ssm_b_im, ssm_c_re, ssm_c_im, ssm_d, ssm_glu_w, w_out,
           ffn2_norm, ffn2_w_gate, ffn2_w_up, ffn2_w_down):
    b, s, _ = h.shape
    h = _ffn(h.reshape(b * s, D_MODEL), ffn1_norm, ffn1_w_gate, ffn1_w_up, ffn1_w_down).reshape(b, s, D_MODEL)
    q, kvc, ks, vs, kw, vw, gates, u = _proj(h, mix_norm, w_in[:, :P_END_SRC], q_norm, k_norm_slc, k_norm_win)
    kc, vc = _compress(kvc, cmp_pos_k, cmp_pos_v, cmp_k_w1, cmp_v_w1, cmp_k_w2, cmp_v_w2, k_norm_cmp)
    o_cmp, qs = _cmp_select(q, kc, vc, gates)
    o_slc = _slc_attn(qs, ks, vs, gates)
    o_win = _win_attn(q, kw, vw, gates)
    ar, ai, bbr, bbi = _ssm_disc(ssm_lambda_re, ssm_lambda_im, ssm_log_step, ssm_b_re, ssm_b_im)
    assert b == SUBLANES
    gy = _ssm_scan(u.transpose(1, 0, 2).reshape(s * b, SSM_WIDTH), ar, ai,
                   _diag_blocks(_group_diag_in(bbr)).astype(BF16), _diag_blocks(_group_diag_in(bbi)).astype(BF16),
                   _diag_blocks(_group_diag_out(ssm_c_re)).astype(BF16),
                   _diag_blocks(_group_diag_out(ssm_c_im)).astype(BF16),
                   ssm_d.reshape(1, SSM_WIDTH))
    h = _merge(h, mix_norm, w_in[:, P_END_SRC:], o_cmp, o_slc, o_win, w_nsa_proj,
               gy.reshape(s, b, SSM_WIDTH).transpose(1, 0, 2), ssm_glu_w, w_out)
    h = _ffn(h.reshape(b * s, D_MODEL), ffn2_norm, ffn2_w_gate, ffn2_w_up, ffn2_w_down)
    return h.reshape(b, s, D_MODEL)


P_END_SRC = NSA_WIDTH + 6 * KV_WIDTH + 3 * NSA_HEADS + SSM_WIDTH


def kernel(x, ffn1_norm, ffn1_w_gate, ffn1_w_up, ffn1_w_down, mix_norm, w_in, q_norm, k_norm_cmp, k_norm_slc, k_norm_win, cmp_pos_k, cmp_pos_v, cmp_k_w1, cmp_k_w2, cmp_v_w1, cmp_v_w2, w_nsa_proj, ssm_lambda_re, ssm_lambda_im, ssm_log_step, ssm_b_re, ssm_b_im, ssm_c_re, ssm_c_im, ssm_d, ssm_glu_w, w_out, ffn2_norm, ffn2_w_gate, ffn2_w_up, ffn2_w_down):
    params = (ffn1_norm, ffn1_w_gate, ffn1_w_up, ffn1_w_down, mix_norm, w_in, q_norm, k_norm_cmp, k_norm_slc,
              k_norm_win, cmp_pos_k, cmp_pos_v, cmp_k_w1, cmp_k_w2, cmp_v_w1, cmp_v_w2, w_nsa_proj,
              ssm_lambda_re, ssm_lambda_im, ssm_log_step, ssm_b_re, ssm_b_im, ssm_c_re, ssm_c_im, ssm_d,
              ssm_glu_w, w_out, ffn2_norm, ffn2_w_gate, ffn2_w_up, ffn2_w_down)
    h = x.astype(F32)
    for layer in range(ffn1_norm.shape[0]):
        h = _layer(h, *[p[layer] for p in params])
    return h.astype(x.dtype)
```

```python
import functools
import math

import jax
import jax.numpy as jnp
from jax import lax
from jax.experimental import pallas as pl
from jax.experimental.pallas import tpu as pltpu

F32 = jnp.float32
BF16 = jnp.bfloat16

D_MODEL = 1024
NSA_HEADS = 8
NSA_KV_GROUPS = 2
HEAD_DIM = 64
Q_PER_KV = NSA_HEADS // NSA_KV_GROUPS
NSA_WIDTH = NSA_HEADS * HEAD_DIM
KV_WIDTH = NSA_KV_GROUPS * HEAD_DIM
CMP_BLOCK = 32
CMP_STRIDE = 16
CMP_HIDDEN = 256
SLC_BLOCK = 64
SLC_TOPK = 16
WINDOW = 512
FORCE_BONUS = 1000.0
SSM_GROUP = 16
SSM_GROUPS = 32
SSM_STATE = 64
SSM_WIDTH = SSM_GROUPS * SSM_GROUP
SSM_STATES = SSM_GROUPS * SSM_STATE
D_FF = 2816
RMS_EPS = 1e-6
NEG = -1e30

LANES = 128
SUBLANES = 8
VMEM_LIMIT = 56 * 1024 * 1024

P_Q = 0
P_KVC = P_Q + NSA_WIDTH
P_KS = P_KVC + 2 * KV_WIDTH
P_VS = P_KS + KV_WIDTH
P_KW = P_VS + KV_WIDTH
P_VW = P_KW + KV_WIDTH
P_GATE = P_VW + KV_WIDTH
P_U = P_GATE + NSA_KV_GROUPS * LANES
P_END = P_U + SSM_WIDTH


def _params(sem, vmem=VMEM_LIMIT):
    return pltpu.CompilerParams(dimension_semantics=sem, vmem_limit_bytes=vmem)


def _const_spec(shape):
    n = len(shape)
    return pl.BlockSpec(shape, lambda *_: (0,) * n, pipeline_mode=pl.Buffered(1))


def _rms(x, g):
    return x * lax.rsqrt(jnp.mean(x * x, axis=-1, keepdims=True) + RMS_EPS) * g


def _slot_rms(x, g):
    ms = jnp.sum(x * x, axis=-1, keepdims=True) * (1.0 / HEAD_DIM)
    return x * lax.rsqrt(ms + RMS_EPS) * g


def _dot(a, b):
    return jnp.dot(a, b, preferred_element_type=F32)


def _dot_nt(a, b):
    return lax.dot_general(a, b, (((1,), (1,)), ((), ())), preferred_element_type=F32)


MXU_DIM = 256
FF_SPLITS = (0, (D_FF // 2 + MXU_DIM - 1) // MXU_DIM * MXU_DIM, D_FF)


def _ffn_body(x_ref, g_ref, wg_ref, wu_ref, wd_ref, o_ref):
    x = x_ref[...]
    hb = _rms(x, g_ref[...]).astype(BF16)
    acc = jnp.zeros_like(x)
    for c in range(len(FF_SPLITS) - 1):
        sl = slice(FF_SPLITS[c], FF_SPLITS[c + 1])
        gate = _dot(hb, wg_ref[:, sl])
        up = _dot(hb, wu_ref[:, sl])
        act = (gate * jax.nn.sigmoid(gate) * up).astype(BF16)
        acc = acc + _dot(act, wd_ref[sl, :])
    o_ref[...] = x + 0.5 * acc


def _ffn(x2, g, wg, wu, wd, tm=512):
    t = x2.shape[0]
    return pl.pallas_call(
        _ffn_body,
        grid=(t // tm,),
        in_specs=[pl.BlockSpec((tm, D_MODEL), lambda i: (i, 0)),
                  _const_spec((1, D_MODEL)),
                  _const_spec((D_MODEL, D_FF)),
                  _const_spec((D_MODEL, D_FF)),
                  _const_spec((D_FF, D_MODEL))],
        out_specs=pl.BlockSpec((tm, D_MODEL), lambda i: (i, 0)),
        out_shape=jax.ShapeDtypeStruct((t, D_MODEL), F32),
        compiler_params=_params(("parallel",)),
        name="ffn",
    )(x2, g.reshape(1, D_MODEL), wg.astype(BF16), wu.astype(BF16), wd.astype(BF16))


def _pair_rms(x, gain, low):
    sq = x * x
    ms_low = jnp.sum(jnp.where(low, sq, 0.0), axis=-1, keepdims=True)
    ms_high = jnp.sum(jnp.where(low, 0.0, sq), axis=-1, keepdims=True)
    ms = jnp.where(low, ms_low, ms_high) * (1.0 / HEAD_DIM)
    return x * lax.rsqrt(ms + RMS_EPS) * gain


def _pair_slots(x, low, fill):
    return jnp.where(low, x, fill), jnp.where(low, pltpu.roll(x, HEAD_DIM, 1), fill)


def _proj_body(h_ref, g_ref, w_ref, qn_ref, ksn_ref, kwn_ref,
               q_ref, kvc_ref, ks_ref, vs_ref, kw_ref, vw_ref, gate_ref, u_ref):
    ts = h_ref.shape[1]
    hb = _rms(h_ref[0], g_ref[...]).astype(BF16)
    p = _dot(hb, w_ref[...])
    lane = lax.broadcasted_iota(jnp.int32, (ts, LANES), 1)
    tok = lax.broadcasted_iota(jnp.int32, (ts, LANES), 0) + pl.program_id(1) * ts
    low = lane < HEAD_DIM
    onehot = jnp.where(lane - HEAD_DIM == tok // SLC_BLOCK, 1.0, 0.0)
    qscale = HEAD_DIM ** -0.5 * math.log2(math.e)
    tile = lambda base, c=0: p[:, base + c * LANES:base + (c + 1) * LANES]
    for c in range(NSA_HEADS // 2):
        pair = _pair_slots(_pair_rms(tile(P_Q, c), qn_ref[...], low) * qscale, low, 0.0)
        q_ref[0, 2 * c] = pair[0].astype(BF16)
        q_ref[0, 2 * c + 1] = pair[1].astype(BF16)
    kvc_ref[0] = p[:, P_KVC:P_KS]
    outs = ((ks_ref, _pair_slots(_pair_rms(tile(P_KS), ksn_ref[...], low), low, onehot)),
            (vs_ref, _pair_slots(tile(P_VS), low, 1.0)),
            (kw_ref, _pair_slots(_pair_rms(tile(P_KW), kwn_ref[...], low), low, 0.0)),
            (vw_ref, _pair_slots(tile(P_VW), low, 1.0)))
    for ref, pair in outs:
        for g in range(NSA_KV_GROUPS):
            ref[0, g] = pair[g].astype(BF16)
    for g in range(NSA_KV_GROUPS):
        gate_ref[0, g] = jax.nn.sigmoid(tile(P_GATE, g))
    u_ref[0] = p[:, P_U:P_END].astype(BF16)


def _slot_gain(g):
    return jnp.pad(g, (0, LANES - HEAD_DIM)).reshape(1, LANES)


def _pair_gain(g):
    return jnp.tile(g, LANES // HEAD_DIM).reshape(1, LANES)


def _proj_weight(w_in):
    n_head_cols = NSA_WIDTH + 6 * KV_WIDTH
    n_gate = 3 * NSA_HEADS
    gate = w_in[:, n_head_cols:n_head_cols + n_gate]
    gate = gate.reshape(D_MODEL, 3, NSA_KV_GROUPS, Q_PER_KV).transpose(0, 2, 1, 3)
    gate = gate.reshape(D_MODEL, NSA_KV_GROUPS, 3 * Q_PER_KV)
    gate = jnp.pad(gate, ((0, 0), (0, 0), (0, LANES - 3 * Q_PER_KV))).reshape(D_MODEL, NSA_KV_GROUPS * LANES)
    cols = [w_in[:, :n_head_cols], gate, w_in[:, n_head_cols + n_gate:]]
    return jnp.concatenate(cols, axis=1).astype(BF16)


def _proj(h, mix_norm, w_in, q_norm, k_norm_slc, k_norm_win, ts=512):
    b, s, _ = h.shape
    hm = lambda n: jax.ShapeDtypeStruct((b, n, s, LANES), BF16)
    hm_spec = lambda n: pl.BlockSpec((1, n, ts, LANES), lambda bi, i: (bi, 0, i, 0))
    row_spec = lambda w: pl.BlockSpec((1, ts, w), lambda bi, i: (bi, i, 0))
    return pl.pallas_call(
        _proj_body,
        grid=(b, s // ts),
        in_specs=[row_spec(D_MODEL), _const_spec((1, D_MODEL)), _const_spec((D_MODEL, P_END)),
                  _const_spec((1, LANES)), _const_spec((1, LANES)), _const_spec((1, LANES))],
        out_specs=[hm_spec(NSA_HEADS), row_spec(2 * KV_WIDTH),
                   hm_spec(NSA_KV_GROUPS), hm_spec(NSA_KV_GROUPS), hm_spec(NSA_KV_GROUPS), hm_spec(NSA_KV_GROUPS),
                   hm_spec(NSA_KV_GROUPS), row_spec(SSM_WIDTH)],
        out_shape=[hm(NSA_HEADS), jax.ShapeDtypeStruct((b, s, 2 * KV_WIDTH), F32),
                   hm(NSA_KV_GROUPS), hm(NSA_KV_GROUPS), hm(NSA_KV_GROUPS), hm(NSA_KV_GROUPS),
                   jax.ShapeDtypeStruct((b, NSA_KV_GROUPS, s, LANES), F32),
                   jax.ShapeDtypeStruct((b, s, SSM_WIDTH), BF16)],
        compiler_params=_params(("parallel", "parallel")),
        name="proj",
    )(h, mix_norm.reshape(1, D_MODEL), _proj_weight(w_in),
      _pair_gain(q_norm), _pair_gain(k_norm_slc), _pair_gain(k_norm_win))


CMP_ROW = CMP_STRIDE * 2 * KV_WIDTH
CMP_HID_ALL = 4 * CMP_HIDDEN


def _compress_body(c_ref, wlo_ref, whi_ref, pk_ref, pv_ref, w1k_ref, w1v_ref, w2_ref, kn_ref,
                   kc_ref, vc_ref):
    nch = c_ref.shape[1]
    c = c_ref[0].astype(BF16)
    first = _dot(c, wlo_ref[...])
    second = _dot(c, whi_ref[...])
    second = pltpu.roll(second, nch - 1, 0)
    rows8 = lambda r: jnp.broadcast_to(r[...].astype(BF16), (SUBLANES, r.shape[1]))
    bk = _dot(rows8(pk_ref), w1k_ref[...])[0:1]
    bv = _dot(rows8(pv_ref), w1v_ref[...])[0:1]
    bias = jnp.concatenate([bk, bk, bv, bv], axis=1)
    hid = jax.nn.gelu(first + second + bias).astype(BF16)
    out = _dot(hid, w2_ref[...])
    lane = lax.broadcasted_iota(jnp.int32, (nch, LANES), 1)
    low = lane < HEAD_DIM
    for g in range(NSA_KV_GROUPS):
        kc_ref[0, g] = _slot_rms(out[:, g * LANES:(g + 1) * LANES], kn_ref[...]).astype(BF16)
        v = out[:, (NSA_KV_GROUPS + g) * LANES:(NSA_KV_GROUPS + g + 1) * LANES]
        vc_ref[0, g] = jnp.where(low, v, 1.0).astype(BF16)


def _compress_weights(w1k, w1v, w2k, w2v):
    def half(w1, lo):
        w = w1[lo * CMP_STRIDE * HEAD_DIM:(lo + 1) * CMP_STRIDE * HEAD_DIM].reshape(CMP_STRIDE, HEAD_DIM, CMP_HIDDEN)
        return w
    eye4 = jnp.eye(4, dtype=F32)
    def build(lo):
        wk, wv = half(w1k, lo), half(w1v, lo)
        w = jnp.stack([wk, wk, wv, wv], axis=1)
        w = w[:, :, :, None, :] * eye4[None, :, None, :, None]
        return w.reshape(CMP_ROW, CMP_HID_ALL).astype(BF16)
    w2 = jnp.stack([w2k, w2k, w2v, w2v], axis=0)
    w2 = jnp.pad(w2, ((0, 0), (0, 0), (0, LANES - HEAD_DIM)))
    w2 = w2[:, :, None, :] * eye4[:, None, :, None]
    return build(0), build(1), w2.reshape(CMP_HID_ALL, 4 * LANES).astype(BF16)


def _compress(kvc, cmp_pos_k, cmp_pos_v, w1k, w1v, w2k, w2v, k_norm_cmp):
    b, s, _ = kvc.shape
    nch = s // CMP_STRIDE
    wlo, whi, w2 = _compress_weights(w1k, w1v, w2k, w2v)
    flat = CMP_BLOCK * HEAD_DIM
    out = jax.ShapeDtypeStruct((b, NSA_KV_GROUPS, nch, LANES), BF16)
    out_spec = pl.BlockSpec((1, NSA_KV_GROUPS, nch, LANES), lambda bi: (bi, 0, 0, 0))
    return pl.pallas_call(
        _compress_body,
        grid=(b,),
        in_specs=[pl.BlockSpec((1, nch, CMP_ROW), lambda bi: (bi, 0, 0)),
                  _const_spec((CMP_ROW, CMP_HID_ALL)), _const_spec((CMP_ROW, CMP_HID_ALL)),
                  _const_spec((1, flat)), _const_spec((1, flat)),
                  _const_spec((flat, CMP_HIDDEN)), _const_spec((flat, CMP_HIDDEN)),
                  _const_spec((CMP_HID_ALL, 4 * LANES)), _const_spec((1, LANES))],
        out_specs=[out_spec, out_spec],
        out_shape=[out, out],
        compiler_params=_params(("parallel",)),
        name="compress",
    )(kvc.reshape(b, nch, CMP_ROW), wlo, whi, cmp_pos_k.reshape(1, flat), cmp_pos_v.reshape(1, flat),
      w1k.astype(BF16), w1v.astype(BF16), w2, _slot_gain(k_norm_cmp))


GROUP_WIDTH = Q_PER_KV * HEAD_DIM


def _store_heads(o_ref, outs):
    o_ref[0] = jnp.concatenate([o[:, :HEAD_DIM] for o in outs], axis=1).astype(BF16)


def _attn_out(b, s, tq):
    spec = pl.BlockSpec((1, tq, GROUP_WIDTH), lambda bi, g, i: (bi, i, g))
    return spec, jax.ShapeDtypeStruct((b, s, NSA_WIDTH), BF16)


def _masked_softmax(s, mask, axis):
    s = jnp.where(mask, s, NEG)
    m = jnp.max(s, axis=axis, keepdims=True)
    e = jnp.exp2(s - m)
    den = jnp.sum(e, axis=axis, keepdims=True)
    return e * jnp.where(m > 0.5 * NEG, 1.0 / den, 0.0)


def _topk_rows(score, k):
    n, t = score.shape
    gone = -3e38

    def peel(_, carry):
        work, cnt, thr = carry
        m = jnp.max(work, axis=0, keepdims=True)
        hit = work == m
        thr = jnp.where(cnt < k, m, thr)
        cnt = cnt + jnp.sum(jnp.where(hit, 1.0, 0.0), axis=0, keepdims=True)
        return jnp.where(hit, gone, work), cnt, thr

    init = (score, jnp.zeros((1, t), F32), jnp.full((1, t), gone, F32))
    _, _, thr = lax.fori_loop(0, k, peel, init, unroll=True)
    above = jnp.where(score > thr, 1.0, 0.0)
    tied = jnp.where(score == thr, 1.0, 0.0)
    room = k - jnp.sum(above, axis=0, keepdims=True)
    lower = lax.broadcasted_iota(jnp.int32, (n, n), 1) < lax.broadcasted_iota(jnp.int32, (n, n), 0)
    tied_before = _dot(jnp.where(lower, 1.0, 0.0).astype(BF16), tied.astype(BF16))
    return jnp.maximum(above, tied * jnp.where(tied_before < room, 1.0, 0.0))


def _cmp_select_body(q_ref, kc_ref, vc_ref, gate_ref, o_ref, qs_ref, *, ns):
    tq = q_ref.shape[2]
    nch = kc_ref.shape[2]
    t0 = pl.program_id(2) * tq
    q = q_ref[0].reshape(Q_PER_KV * tq, LANES)
    s = _dot_nt(q, kc_ref[0, 0]).reshape(Q_PER_KV, tq, nch)
    tok = t0 + lax.broadcasted_iota(jnp.int32, (tq, nch), 0)
    cid = lax.broadcasted_iota(jnp.int32, (tq, nch), 1)
    p = _masked_softmax(s, (cid * CMP_STRIDE + CMP_BLOCK - 1 <= tok)[None], axis=-1)
    o = _dot(p.reshape(Q_PER_KV * tq, nch).astype(BF16), vc_ref[0, 0]).reshape(Q_PER_KV, tq, LANES)
    gate = gate_ref[0, 0]
    _store_heads(o_ref, [o[r] * gate[:, r:r + 1] for r in range(Q_PER_KV)])
    psum = p[0] + p[1] + p[2] + p[3]
    cb = lax.broadcasted_iota(jnp.int32, (nch, LANES), 0)
    jb = lax.broadcasted_iota(jnp.int32, (nch, LANES), 1)
    overlap = jnp.where((cb * CMP_STRIDE < (jb + 1) * SLC_BLOCK) & (cb * CMP_STRIDE + CMP_BLOCK > jb * SLC_BLOCK), 1.0, 0.0)
    imp = _dot(psum, overlap).T[:ns]
    j = lax.broadcasted_iota(jnp.int32, (ns, tq), 0)
    qblk = (t0 + lax.broadcasted_iota(jnp.int32, (ns, tq), 1)) // SLC_BLOCK
    force = (j == 0) | (j == qblk) | (j == qblk - 1)
    score = jnp.where(j <= qblk, imp + FORCE_BONUS * jnp.where(force, 1.0, 0.0), NEG)
    sel_bias = jnp.where(_topk_rows(score, SLC_TOPK) > 0, 0.0, NEG)
    pieces = [jnp.zeros((HEAD_DIM, tq), F32), sel_bias]
    if ns < LANES - HEAD_DIM:
        pieces.append(jnp.zeros((LANES - HEAD_DIM - ns, tq), F32))
    bias = jnp.concatenate(pieces, axis=0).T
    for r in range(Q_PER_KV):
        qs_ref[0, r] = (q_ref[0, r].astype(F32) + bias).astype(BF16)


def _cmp_select(q, kc, vc, gates, tq=256):
    b, _, s, _ = q.shape
    nch = kc.shape[2]
    ns = s // SLC_BLOCK
    assert ns <= LANES - HEAD_DIM
    qspec = pl.BlockSpec((1, Q_PER_KV, tq, LANES), lambda bi, g, i: (bi, g, i, 0))
    cspec = pl.BlockSpec((1, 1, nch, LANES), lambda bi, g, i: (bi, g, 0, 0))
    ospec, oshape = _attn_out(b, s, tq)
    return pl.pallas_call(
        functools.partial(_cmp_select_body, ns=ns),
        grid=(b, NSA_KV_GROUPS, s // tq),
        in_specs=[qspec, cspec, cspec, pl.BlockSpec((1, 1, tq, LANES), lambda bi, g, i: (bi, g, i, 0))],
        out_specs=[ospec, qspec],
        out_shape=[oshape, jax.ShapeDtypeStruct((b, NSA_HEADS, s, LANES), BF16)],
        compiler_params=_params(("parallel", "parallel", "parallel")),
        name="cmp_select",
    )(q, kc, vc, gates)


def _slc_body(q_ref, k_ref, v_ref, gate_ref, o_ref, p_ref, m_ref, acc_ref):
    tq = q_ref.shape[2]
    i = pl.program_id(2)
    n_lane_tiles = tq // LANES

    def scores(r, j):
        k0 = pl.multiple_of(j * tq, tq)
        return _dot_nt(q_ref[0, r], k_ref[0, 0, pl.ds(k0, tq), :]).astype(BF16)

    def pending_values(r, j):
        k0 = pl.multiple_of(j * tq, tq)
        return _dot(p_ref[r], v_ref[0, 0, pl.ds(k0, tq), :])

    def row_max(s):
        part = functools.reduce(jnp.maximum, [s[:, c * LANES:(c + 1) * LANES] for c in range(n_lane_tiles)])
        return jnp.broadcast_to(jnp.max(part, axis=-1, keepdims=True), (tq, LANES))

    def probabilities(s, m):
        return jnp.concatenate([jnp.exp2(s[:, c * LANES:(c + 1) * LANES] - m) for c in range(n_lane_tiles)], axis=1)

    keep = lax.broadcasted_iota(jnp.int32, (tq, tq), 1) <= lax.broadcasted_iota(jnp.int32, (tq, tq), 0)
    for r in range(Q_PER_KV):
        s = jnp.where(keep, scores(r, i), NEG)
        m = row_max(s)
        m_ref[r] = m
        p_ref[r] = probabilities(s, m)
        acc_ref[r] = jnp.zeros((tq, LANES), F32)

    def step(j, carry):
        j_pend = jnp.where(j == 0, i, j - 1)
        for r in range(Q_PER_KV):
            pv = pending_values(r, j_pend)
            s = scores(r, j)
            m = m_ref[r]
            m_new = jnp.maximum(m, row_max(s))
            p_ref[r] = probabilities(s, m_new)
            alpha = jnp.exp2(m.astype(F32) - m_new.astype(F32))
            acc_ref[r] = alpha * (acc_ref[r] + pv)
            m_ref[r] = m_new
        return carry

    lax.fori_loop(0, i, step, 0)
    gate = gate_ref[0, 0]
    outs = []
    for r in range(Q_PER_KV):
        acc = acc_ref[r] + pending_values(r, jnp.where(i == 0, 0, i - 1))
        o = acc / acc[:, HEAD_DIM:HEAD_DIM + 1]
        outs.append(o * gate[:, Q_PER_KV + r:Q_PER_KV + r + 1])
    _store_heads(o_ref, outs)


def _slc_attn(qs, k, v, gates, tq=512):
    b, _, s, _ = qs.shape
    qspec = pl.BlockSpec((1, Q_PER_KV, tq, LANES), lambda bi, g, i: (bi, g, i, 0))
    kspec = pl.BlockSpec((1, 1, s, LANES), lambda bi, g, i: (bi, g, 0, 0))
    ospec, oshape = _attn_out(b, s, tq)
    return pl.pallas_call(
        _slc_body,
        grid=(b, NSA_KV_GROUPS, s // tq),
        in_specs=[qspec, kspec, kspec, pl.BlockSpec((1, 1, tq, LANES), lambda bi, g, i: (bi, g, i, 0))],
        out_specs=ospec,
        out_shape=oshape,
        scratch_shapes=[pltpu.VMEM((Q_PER_KV, tq, tq), BF16), pltpu.VMEM((Q_PER_KV, tq, LANES), BF16),
                        pltpu.VMEM((Q_PER_KV, tq, LANES), F32)],
        compiler_params=_params(("parallel", "parallel", "arbitrary")),
        name="slc_attn",
    )(qs, k, v, gates)


def _win_body(q_ref, k_ref, v_ref, gate_ref, o_ref):
    tq = q_ref.shape[2]
    nprev = WINDOW // tq
    i = pl.program_id(2)
    qpos = i * tq + lax.broadcasted_iota(jnp.int32, (tq, tq), 0)
    col = lax.broadcasted_iota(jnp.int32, (tq, tq), 1)
    keeps, starts = [], []
    for d in range(nprev + 1):
        jt = i - nprev + d
        diff = qpos - (jt * tq + col)
        keeps.append((diff >= 0) & (diff < WINDOW) & (jt >= 0))
        starts.append(pl.multiple_of(jnp.maximum(jt, 0) * tq, tq))
    gate = gate_ref[0, 0]

    outs = []
    for r in range(Q_PER_KV):
        q = q_ref[0, r]
        s = [jnp.where(keep, _dot_nt(q, k_ref[0, 0, pl.ds(k0, tq), :]).astype(BF16), NEG)
             for keep, k0 in zip(keeps, starts)]
        m = jnp.max(functools.reduce(jnp.maximum, s), axis=-1, keepdims=True)
        acc = sum(_dot(jnp.exp2(sd - m), v_ref[0, 0, pl.ds(k0, tq), :]) for sd, k0 in zip(s, starts))
        o = acc / acc[:, HEAD_DIM:HEAD_DIM + 1]
        outs.append(o * gate[:, 2 * Q_PER_KV + r:2 * Q_PER_KV + r + 1])
    _store_heads(o_ref, outs)


def _win_attn(q, k, v, gates, tq=512):
    b, _, s, _ = q.shape
    assert WINDOW % tq == 0
    qspec = pl.BlockSpec((1, Q_PER_KV, tq, LANES), lambda bi, g, i: (bi, g, i, 0))
    kspec = pl.BlockSpec((1, 1, s, LANES), lambda bi, g, i: (bi, g, 0, 0))
    ospec, oshape = _attn_out(b, s, tq)
    return pl.pallas_call(
        _win_body,
        grid=(b, NSA_KV_GROUPS, s // tq),
        in_specs=[qspec, kspec, kspec, pl.BlockSpec((1, 1, tq, LANES), lambda bi, g, i: (bi, g, i, 0))],
        out_specs=ospec,
        out_shape=oshape,
        compiler_params=_params(("parallel", "parallel", "arbitrary")),
        name="win_attn",
    )(q, k, v, gates)


def _cmp_win_body(q_ref, kc_ref, vc_ref, kw_ref, vw_ref, gate_ref, ocmp_ref, qs_ref, owin_ref, *, ns):
    _cmp_select_body(q_ref, kc_ref, vc_ref, gate_ref, ocmp_ref, qs_ref, ns=ns)
    _win_body(q_ref, kw_ref, vw_ref, gate_ref, owin_ref)


def _cmp_win(q, kc, vc, kw, vw, gates, tq=512):
    b, _, s, _ = q.shape
    nch = kc.shape[2]
    ns = s // SLC_BLOCK
    assert ns <= LANES - HEAD_DIM and WINDOW % tq == 0
    qspec = pl.BlockSpec((1, Q_PER_KV, tq, LANES), lambda bi, g, i: (bi, g, i, 0))
    cspec = pl.BlockSpec((1, 1, nch, LANES), lambda bi, g, i: (bi, g, 0, 0))
    kspec = pl.BlockSpec((1, 1, s, LANES), lambda bi, g, i: (bi, g, 0, 0))
    ospec, oshape = _attn_out(b, s, tq)
    return pl.pallas_call(
        functools.partial(_cmp_win_body, ns=ns),
        grid=(b, NSA_KV_GROUPS, s // tq),
        in_specs=[qspec, cspec, cspec, kspec, kspec,
                  pl.BlockSpec((1, 1, tq, LANES), lambda bi, g, i: (bi, g, i, 0))],
        out_specs=[ospec, qspec, ospec],
        out_shape=[oshape, jax.ShapeDtypeStruct((b, NSA_HEADS, s, LANES), BF16), oshape],
        compiler_params=_params(("parallel", "parallel", "arbitrary")),
        name="cmp_win",
    )(q, kc, vc, kw, vw, gates)


def _ssm_disc_body(lr_ref, li_ref, ls_ref, br_ref, bi_ref, ar_ref, ai_ref, bbr_ref, bbi_ref):
    lr, li = lr_ref[...], li_ref[...]
    step = jnp.exp(ls_ref[...])
    mag = jnp.exp(lr * step)
    ar = mag * jnp.cos(li * step)
    ai = mag * jnp.sin(li * step)
    den = lr * lr + li * li
    cr = ((ar - 1.0) * lr + ai * li) / den
    ci = (ai * lr - (ar - 1.0) * li) / den
    br, bi = br_ref[...], bi_ref[...]
    ar_ref[...] = jnp.broadcast_to(ar, ar_ref.shape)
    ai_ref[...] = jnp.broadcast_to(ai, ai_ref.shape)
    bbr_ref[...] = cr * br - ci * bi
    bbi_ref[...] = cr * bi + ci * br


def _ssm_disc(lam_re, lam_im, log_step, b_re, b_im):
    row = lambda a: a.reshape(1, SSM_STATES)
    chan = lambda a: a.transpose(2, 0, 1).reshape(SSM_GROUP, SSM_STATES)
    ls = jnp.broadcast_to(log_step[:, None], (SSM_GROUPS, SSM_STATE))
    return pl.pallas_call(
        _ssm_disc_body,
        out_shape=[jax.ShapeDtypeStruct((SUBLANES, SSM_STATES), F32)] * 2
        + [jax.ShapeDtypeStruct((SSM_GROUP, SSM_STATES), F32)] * 2,
        name="ssm_disc",
    )(row(lam_re), row(lam_im), row(ls), chan(b_re), chan(b_im))


def _group_diag_in(w):
    w = w.reshape(SSM_GROUP, SSM_GROUPS, SSM_STATE)
    eye = jnp.eye(SSM_GROUPS, dtype=w.dtype)
    return (w[None] * eye[:, None, :, None]).reshape(SSM_WIDTH, SSM_STATES)


def _group_diag_out(c):
    eye = jnp.eye(SSM_GROUPS, dtype=c.dtype)
    return (c.transpose(0, 2, 1)[:, :, None, :] * eye[:, None, :, None]).reshape(SSM_STATES, SSM_WIDTH)


SCAN_COLS = 512
SSM_BLOCKS = SSM_STATES // SCAN_COLS
SCAN_CHANS = SSM_WIDTH // SSM_BLOCKS


def _diag_blocks(w):
    r, c = w.shape[0] // SSM_BLOCKS, w.shape[1] // SSM_BLOCKS
    return jnp.stack([w[m * r:(m + 1) * r, m * c:(m + 1) * c] for m in range(SSM_BLOCKS)])


def _ssm_scan_body(u_ref, ar_ref, ai_ref, wbr_ref, wbi_ref, wcr_ref, wci_ref, d_ref, o_ref,
                   xr_ref, xi_ref, vr_ref, vi_ref):
    rows = u_ref.shape[0]
    steps = rows // SUBLANES

    @pl.when(pl.program_id(0) == 0)
    def _():
        xr_ref[...] = jnp.zeros_like(xr_ref)
        xi_ref[...] = jnp.zeros_like(xi_ref)

    ub = u_ref[...]
    u = ub.astype(F32)
    ys = []
    for cb in range(SSM_BLOCKS):
        cs = slice(cb * SCAN_COLS, (cb + 1) * SCAN_COLS)
        ubc = ub[:, cb * SCAN_CHANS:(cb + 1) * SCAN_CHANS]
        vr_ref[:, cs] = _dot(ubc, wbr_ref[cb])
        vi_ref[:, cs] = _dot(ubc, wbi_ref[cb])
        ar, ai = ar_ref[:, cs], ai_ref[:, cs]

        def step(t, carry):
            xr, xi = carry
            r0 = pl.multiple_of(t * SUBLANES, SUBLANES)
            nr = ar * xr - ai * xi + vr_ref[pl.ds(r0, SUBLANES), cs]
            ni = ar * xi + ai * xr + vi_ref[pl.ds(r0, SUBLANES), cs]
            vr_ref[pl.ds(r0, SUBLANES), cs] = nr
            vi_ref[pl.ds(r0, SUBLANES), cs] = ni
            return nr, ni

        xr, xi = lax.fori_loop(0, steps, step, (xr_ref[:, cs], xi_ref[:, cs]), unroll=8)
        xr_ref[:, cs] = xr
        xi_ref[:, cs] = xi
        ys.append(_dot(vr_ref[:, cs].astype(BF16), wcr_ref[cb]) - _dot(vi_ref[:, cs].astype(BF16), wci_ref[cb]))
    y = jnp.concatenate(ys, axis=1) + d_ref[...] * u
    o_ref[...] = jax.nn.gelu(y).astype(BF16)


def _ssm_scan(u_tm, ar, ai, wbr, wbi, wcr, wci, d, steps=64):
    rows = steps * SUBLANES
    n = u_tm.shape[0]
    return pl.pallas_call(
        _ssm_scan_body,
        grid=(n // rows,),
        in_specs=[pl.BlockSpec((rows, SSM_WIDTH), lambda i: (i, 0)),
                  _const_spec((SUBLANES, SSM_STATES)), _const_spec((SUBLANES, SSM_STATES)),
                  _const_spec((SSM_BLOCKS, SCAN_CHANS, SCAN_COLS)), _const_spec((SSM_BLOCKS, SCAN_CHANS, SCAN_COLS)),
                  _const_spec((SSM_BLOCKS, SCAN_COLS, SCAN_CHANS)), _const_spec((SSM_BLOCKS, SCAN_COLS, SCAN_CHANS)),
                  _const_spec((1, SSM_WIDTH))],
        out_specs=pl.BlockSpec((rows, SSM_WIDTH), lambda i: (i, 0)),
        out_shape=jax.ShapeDtypeStruct((n, SSM_WIDTH), BF16),
        scratch_shapes=[pltpu.VMEM((SUBLANES, SSM_STATES), F32), pltpu.VMEM((SUBLANES, SSM_STATES), F32),
                        pltpu.VMEM((rows, SSM_STATES), F32), pltpu.VMEM((rows, SSM_STATES), F32)],
        compiler_params=_params(("arbitrary",)),
        name="ssm_scan",
    )(u_tm, ar, ai, wbr, wbi, wcr, wci, d)


def _merge_body(h_ref, g_ref, wgate_ref, oc_ref, os_ref, ow_ref, wn_ref, gy_ref, wglu_ref, wout_ref, o_ref):
    h = h_ref[0]
    hb = _rms(h, g_ref[...]).astype(BF16)
    gates = jax.nn.sigmoid(_dot(hb, wgate_ref[...]))
    o = oc_ref[0].astype(F32) + os_ref[0].astype(F32) + ow_ref[0].astype(F32)
    y_nsa = _dot(o.astype(BF16), wn_ref[...])
    hg = _dot(gy_ref[0], wglu_ref[...])
    y_ssm = hg[:, :D_MODEL] * jax.nn.sigmoid(hg[:, D_MODEL:])
    merged = gates[:, :D_MODEL] * y_nsa + gates[:, D_MODEL:] * y_ssm
    o_ref[0] = h + _dot(merged.astype(BF16), wout_ref[...])


def _merge(h, mix_norm, w_gates, o_cmp, o_slc, o_win, w_nsa_proj, gy, glu_w, w_out, ts=512):
    b, s, _ = h.shape
    row_spec = lambda w: pl.BlockSpec((1, ts, w), lambda bi, i: (bi, i, 0))
    ospec = row_spec(NSA_WIDTH)
    return pl.pallas_call(
        _merge_body,
        grid=(b, s // ts),
        in_specs=[row_spec(D_MODEL), _const_spec((1, D_MODEL)), _const_spec((D_MODEL, 2 * D_MODEL)),
                  ospec, ospec, ospec, _const_spec((NSA_WIDTH, D_MODEL)),
                  row_spec(SSM_WIDTH), _const_spec((SSM_WIDTH, 2 * D_MODEL)), _const_spec((D_MODEL, D_MODEL))],
        out_specs=row_spec(D_MODEL),
        out_shape=jax.ShapeDtypeStruct((b, s, D_MODEL), F32),
        compiler_params=_params(("parallel", "parallel")),
        name="merge",
    )(h, mix_norm.reshape(1, D_MODEL), w_gates.astype(BF16), o_cmp, o_slc, o_win, w_nsa_proj.astype(BF16),
      gy, glu_w.astype(BF16), w_out.astype(BF16))


def _layer(h, ffn1_norm, ffn1_w_gate, ffn1_w_up, ffn1_w_down, mix_norm, w_in, q_norm,
           k_norm_cmp, k_norm_slc, k_norm_win, cmp_pos_k, cmp_pos_v, cmp_k_w1, cmp_k_w2,
           cmp_v_w1, cmp_v_w2, w_nsa_proj, ssm_lambda_re, ssm_lambda_im, ssm_log_step,
           ssm_b_re, ssm_b_im, ssm_c_re, ssm_c_im, ssm_d, ssm_glu_w, w_out,
           ffn2_norm, ffn2_w_gate, ffn2_w_up, ffn2_w_down):
    b, s, _ = h.shape
    h = _ffn(h.reshape(b * s, D_MODEL), ffn1_norm, ffn1_w_gate, ffn1_w_up, ffn1_w_down).reshape(b, s, D_MODEL)
    q, kvc, ks, vs, kw, vw, gates, u = _proj(h, mix_norm, w_in[:, :P_END_SRC], q_norm, k_norm_slc, k_norm_win)
    kc, vc = _compress(kvc, cmp_pos_k, cmp_pos_v, cmp_k_w1, cmp_v_w1, cmp_k_w2, cmp_v_w2, k_norm_cmp)
    o_cmp, qs, o_win = _cmp_win(q, kc, vc, kw, vw, gates)
    o_slc = _slc_attn(qs, ks, vs, gates)
    ar, ai, bbr, bbi = _ssm_disc(ssm_lambda_re, ssm_lambda_im, ssm_log_step, ssm_b_re, ssm_b_im)
    assert b == SUBLANES
    gy = _ssm_scan(u.transpose(1, 0, 2).reshape(s * b, SSM_WIDTH), ar, ai,
                   _diag_blocks(_group_diag_in(bbr)).astype(BF16), _diag_blocks(_group_diag_in(bbi)).astype(BF16),
                   _diag_blocks(_group_diag_out(ssm_c_re)).astype(BF16),
                   _diag_blocks(_group_diag_out(ssm_c_im)).astype(BF16),
                   ssm_d.reshape(1, SSM_WIDTH))
    h = _merge(h, mix_norm, w_in[:, P_END_SRC:], o_cmp, o_slc, o_win, w_nsa_proj,
               gy.reshape(s, b, SSM_WIDTH).transpose(1, 0, 2), ssm_glu_w, w_out)
    h = _ffn(h.reshape(b * s, D_MODEL), ffn2_norm, ffn2_w_gate, ffn2_w_up, ffn2_w_down)
    return h.reshape(b, s, D_MODEL)


P_END_SRC = NSA_WIDTH + 6 * KV_WIDTH + 3 * NSA_HEADS + SSM_WIDTH


def kernel(x, ffn1_norm, ffn1_w_gate, ffn1_w_up, ffn1_w_down, mix_norm, w_in, q_norm, k_norm_cmp, k_norm_slc, k_norm_win, cmp_pos_k, cmp_pos_v, cmp_k_w1, cmp_k_w2, cmp_v_w1, cmp_v_w2, w_nsa_proj, ssm_lambda_re, ssm_lambda_im, ssm_log_step, ssm_b_re, ssm_b_im, ssm_c_re, ssm_c_im, ssm_d, ssm_glu_w, w_out, ffn2_norm, ffn2_w_gate, ffn2_w_up, ffn2_w_down):
    params = (ffn1_norm, ffn1_w_gate, ffn1_w_up, ffn1_w_down, mix_norm, w_in, q_norm, k_norm_cmp, k_norm_slc,
              k_norm_win, cmp_pos_k, cmp_pos_v, cmp_k_w1, cmp_k_w2, cmp_v_w1, cmp_v_w2, w_nsa_proj,
              ssm_lambda_re, ssm_lambda_im, ssm_log_step, ssm_b_re, ssm_b_im, ssm_c_re, ssm_c_im, ssm_d,
              ssm_glu_w, w_out, ffn2_norm, ffn2_w_gate, ffn2_w_up, ffn2_w_down)
    h = x.astype(F32)
    for layer in range(ffn1_norm.shape[0]):
        h = _layer(h, *[p[layer] for p in params])
    return h.astype(x.dtype)
```

```python
import functools
import math

import jax
import jax.numpy as jnp
from jax import lax
from jax.experimental import pallas as pl
from jax.experimental.pallas import tpu as pltpu

F32 = jnp.float32
BF16 = jnp.bfloat16

D_MODEL = 1024
NSA_HEADS = 8
NSA_KV_GROUPS = 2
HEAD_DIM = 64
Q_PER_KV = NSA_HEADS // NSA_KV_GROUPS
NSA_WIDTH = NSA_HEADS * HEAD_DIM
KV_WIDTH = NSA_KV_GROUPS * HEAD_DIM
CMP_BLOCK = 32
CMP_STRIDE = 16
CMP_HIDDEN = 256
SLC_BLOCK = 64
SLC_TOPK = 16
WINDOW = 512
FORCE_BONUS = 1000.0
SSM_GROUP = 16
SSM_GROUPS = 32
SSM_STATE = 64
SSM_WIDTH = SSM_GROUPS * SSM_GROUP
SSM_STATES = SSM_GROUPS * SSM_STATE
D_FF = 2816
RMS_EPS = 1e-6
NEG = -1e30

LANES = 128
SUBLANES = 8
VMEM_LIMIT = 56 * 1024 * 1024

P_Q = 0
P_KVC = P_Q + NSA_WIDTH
P_KS = P_KVC + 2 * KV_WIDTH
P_VS = P_KS + KV_WIDTH
P_KW = P_VS + KV_WIDTH
P_VW = P_KW + KV_WIDTH
P_GATE = P_VW + KV_WIDTH
P_U = P_GATE + NSA_KV_GROUPS * LANES
P_END = P_U + SSM_WIDTH


def _params(sem, vmem=VMEM_LIMIT):
    return pltpu.CompilerParams(dimension_semantics=sem, vmem_limit_bytes=vmem)


def _const_spec(shape):
    n = len(shape)
    return pl.BlockSpec(shape, lambda *_: (0,) * n, pipeline_mode=pl.Buffered(1))


def _rms(x, g):
    return x * lax.rsqrt(jnp.mean(x * x, axis=-1, keepdims=True) + RMS_EPS) * g


def _slot_rms(x, g):
    ms = jnp.sum(x * x, axis=-1, keepdims=True) * (1.0 / HEAD_DIM)
    return x * lax.rsqrt(ms + RMS_EPS) * g


def _dot(a, b):
    return jnp.dot(a, b, preferred_element_type=F32)


def _dot_nt(a, b):
    return lax.dot_general(a, b, (((1,), (1,)), ((), ())), preferred_element_type=F32)


MXU_DIM = 256
FF_SPLITS = (0, (D_FF // 2 + MXU_DIM - 1) // MXU_DIM * MXU_DIM, D_FF)


def _ffn_body(x_ref, g_ref, wg_ref, wu_ref, wd_ref, o_ref):
    x = x_ref[...]
    hb = _rms(x, g_ref[...]).astype(BF16)
    acc = jnp.zeros_like(x)
    for c in range(len(FF_SPLITS) - 1):
        sl = slice(FF_SPLITS[c], FF_SPLITS[c + 1])
        gate = _dot(hb, wg_ref[:, sl])
        up = _dot(hb, wu_ref[:, sl])
        act = (gate * jax.nn.sigmoid(gate) * up).astype(BF16)
        acc = acc + _dot(act, wd_ref[sl, :])
    o_ref[...] = x + 0.5 * acc


def _ffn(x2, g, wg, wu, wd, tm=512):
    t = x2.shape[0]
    return pl.pallas_call(
        _ffn_body,
        grid=(t // tm,),
        in_specs=[pl.BlockSpec((tm, D_MODEL), lambda i: (i, 0)),
                  _const_spec((1, D_MODEL)),
                  _const_spec((D_MODEL, D_FF)),
                  _const_spec((D_MODEL, D_FF)),
                  _const_spec((D_FF, D_MODEL))],
        out_specs=pl.BlockSpec((tm, D_MODEL), lambda i: (i, 0)),
        out_shape=jax.ShapeDtypeStruct((t, D_MODEL), F32),
        compiler_params=_params(("parallel",)),
        name="ffn",
    )(x2, g.reshape(1, D_MODEL), wg.astype(BF16), wu.astype(BF16), wd.astype(BF16))


def _pair_rms(x, gain, low):
    sq = x * x
    ms_low = jnp.sum(jnp.where(low, sq, 0.0), axis=-1, keepdims=True)
    ms_high = jnp.sum(jnp.where(low, 0.0, sq), axis=-1, keepdims=True)
    ms = jnp.where(low, ms_low, ms_high) * (1.0 / HEAD_DIM)
    return x * lax.rsqrt(ms + RMS_EPS) * gain


def _pair_slots(x, low, fill):
    return jnp.where(low, x, fill), jnp.where(low, pltpu.roll(x, HEAD_DIM, 1), fill)


def _proj_body(h_ref, g_ref, w_ref, qn_ref, ksn_ref, kwn_ref,
               q_ref, kvc_ref, ks_ref, vs_ref, kw_ref, vw_ref, gate_ref, u_ref, kv_sc):
    ts = h_ref.shape[1]
    hb = _rms(h_ref[0], g_ref[...]).astype(BF16)
    p = _dot(hb, w_ref[...])
    lane = lax.broadcasted_iota(jnp.int32, (ts, LANES), 1)
    tok = lax.broadcasted_iota(jnp.int32, (ts, LANES), 0) + pl.program_id(1) * ts
    low = lane < HEAD_DIM
    onehot = jnp.where(lane - HEAD_DIM == tok // SLC_BLOCK, 1.0, 0.0)
    qscale = HEAD_DIM ** -0.5 * math.log2(math.e)
    tile = lambda base, c=0: p[:, base + c * LANES:base + (c + 1) * LANES]
    for c in range(NSA_HEADS // 2):
        pair = _pair_slots(_pair_rms(tile(P_Q, c), qn_ref[...], low) * qscale, low, 0.0)
        q_ref[0, 2 * c] = pair[0].astype(BF16)
        q_ref[0, 2 * c + 1] = pair[1].astype(BF16)
    n_planes = 2 * KV_WIDTH // LANES
    for c in range(n_planes):
        kv_sc[c] = tile(P_KVC, c)
    for j in range(CMP_STRIDE):
        for c in range(n_planes):
            lane0 = (j * n_planes + c) * LANES
            kvc_ref[0, :, lane0:lane0 + LANES] = kv_sc[c, pl.ds(j, ts // CMP_STRIDE, stride=CMP_STRIDE), :]
    outs = ((ks_ref, _pair_slots(_pair_rms(tile(P_KS), ksn_ref[...], low), low, onehot)),
            (vs_ref, _pair_slots(tile(P_VS), low, 1.0)),
            (kw_ref, _pair_slots(_pair_rms(tile(P_KW), kwn_ref[...], low), low, 0.0)),
            (vw_ref, _pair_slots(tile(P_VW), low, 1.0)))
    for ref, pair in outs:
        for g in range(NSA_KV_GROUPS):
            ref[0, g] = pair[g].astype(BF16)
    for g in range(NSA_KV_GROUPS):
        gate_ref[0, g] = jax.nn.sigmoid(tile(P_GATE, g))
    u_ref[...] = p[:, P_U:P_END].astype(BF16)


def _slot_gain(g):
    return jnp.pad(g, (0, LANES - HEAD_DIM)).reshape(1, LANES)


def _pair_gain(g):
    return jnp.tile(g, LANES // HEAD_DIM).reshape(1, LANES)


def _proj_weight(w_in):
    n_head_cols = NSA_WIDTH + 6 * KV_WIDTH
    n_gate = 3 * NSA_HEADS
    gate = w_in[:, n_head_cols:n_head_cols + n_gate]
    gate = gate.reshape(D_MODEL, 3, NSA_KV_GROUPS, Q_PER_KV).transpose(0, 2, 1, 3)
    gate = gate.reshape(D_MODEL, NSA_KV_GROUPS, 3 * Q_PER_KV)
    gate = jnp.pad(gate, ((0, 0), (0, 0), (0, LANES - 3 * Q_PER_KV))).reshape(D_MODEL, NSA_KV_GROUPS * LANES)
    cols = [w_in[:, :n_head_cols], gate, w_in[:, n_head_cols + n_gate:]]
    return jnp.concatenate(cols, axis=1).astype(BF16)


def _proj(h, mix_norm, w_in, q_norm, k_norm_slc, k_norm_win, ts=512):
    b, s, _ = h.shape
    hm = lambda n: jax.ShapeDtypeStruct((b, n, s, LANES), BF16)
    hm_spec = lambda n: pl.BlockSpec((1, n, ts, LANES), lambda bi, i: (bi, 0, i, 0))
    row_spec = lambda w: pl.BlockSpec((1, ts, w), lambda bi, i: (bi, i, 0))
    return pl.pallas_call(
        _proj_body,
        grid=(b, s // ts),
        in_specs=[row_spec(D_MODEL), _const_spec((1, D_MODEL)), _const_spec((D_MODEL, P_END)),
                  _const_spec((1, LANES)), _const_spec((1, LANES)), _const_spec((1, LANES))],
        out_specs=[hm_spec(NSA_HEADS),
                   pl.BlockSpec((1, ts // CMP_STRIDE, CMP_ROW), lambda bi, i: (bi, i, 0)),
                   hm_spec(NSA_KV_GROUPS), hm_spec(NSA_KV_GROUPS), hm_spec(NSA_KV_GROUPS), hm_spec(NSA_KV_GROUPS),
                   hm_spec(NSA_KV_GROUPS),
                   pl.BlockSpec((ts, SSM_WIDTH), lambda bi, i: (i, bi))],
        out_shape=[hm(NSA_HEADS), jax.ShapeDtypeStruct((b, s // CMP_STRIDE, CMP_ROW), F32),
                   hm(NSA_KV_GROUPS), hm(NSA_KV_GROUPS), hm(NSA_KV_GROUPS), hm(NSA_KV_GROUPS),
                   jax.ShapeDtypeStruct((b, NSA_KV_GROUPS, s, LANES), F32),
                   jax.ShapeDtypeStruct((s, b * SSM_WIDTH), BF16)],
        scratch_shapes=[pltpu.VMEM((2 * KV_WIDTH // LANES, ts, LANES), F32)],
        compiler_params=_params(("parallel", "parallel")),
        name="proj",
    )(h, mix_norm.reshape(1, D_MODEL), _proj_weight(w_in),
      _pair_gain(q_norm), _pair_gain(k_norm_slc), _pair_gain(k_norm_win))


CMP_ROW = CMP_STRIDE * 2 * KV_WIDTH
CMP_HID_ALL = 4 * CMP_HIDDEN


def _compress_body(c_ref, wlo_ref, whi_ref, pk_ref, pv_ref, w1k_ref, w1v_ref, w2_ref, kn_ref,
                   kc_ref, vc_ref):
    nch = c_ref.shape[1]
    c = c_ref[0].astype(BF16)
    first = _dot(c, wlo_ref[...])
    second = _dot(c, whi_ref[...])
    second = pltpu.roll(second, nch - 1, 0)
    rows8 = lambda r: jnp.broadcast_to(r[...].astype(BF16), (SUBLANES, r.shape[1]))
    bk = _dot(rows8(pk_ref), w1k_ref[...])[0:1]
    bv = _dot(rows8(pv_ref), w1v_ref[...])[0:1]
    bias = jnp.concatenate([bk, bk, bv, bv], axis=1)
    hid = jax.nn.gelu(first + second + bias).astype(BF16)
    out = _dot(hid, w2_ref[...])
    lane = lax.broadcasted_iota(jnp.int32, (nch, LANES), 1)
    low = lane < HEAD_DIM
    for g in range(NSA_KV_GROUPS):
        kc_ref[0, g] = _slot_rms(out[:, g * LANES:(g + 1) * LANES], kn_ref[...]).astype(BF16)
        v = out[:, (NSA_KV_GROUPS + g) * LANES:(NSA_KV_GROUPS + g + 1) * LANES]
        vc_ref[0, g] = jnp.where(low, v, 1.0).astype(BF16)


def _compress_weights(w1k, w1v, w2k, w2v):
    def half(w1, lo):
        w = w1[lo * CMP_STRIDE * HEAD_DIM:(lo + 1) * CMP_STRIDE * HEAD_DIM].reshape(CMP_STRIDE, HEAD_DIM, CMP_HIDDEN)
        return w
    eye4 = jnp.eye(4, dtype=F32)
    def build(lo):
        wk, wv = half(w1k, lo), half(w1v, lo)
        w = jnp.stack([wk, wk, wv, wv], axis=1)
        w = w[:, :, :, None, :] * eye4[None, :, None, :, None]
        return w.reshape(CMP_ROW, CMP_HID_ALL).astype(BF16)
    w2 = jnp.stack([w2k, w2k, w2v, w2v], axis=0)
    w2 = jnp.pad(w2, ((0, 0), (0, 0), (0, LANES - HEAD_DIM)))
    w2 = w2[:, :, None, :] * eye4[:, None, :, None]
    return build(0), build(1), w2.reshape(CMP_HID_ALL, 4 * LANES).astype(BF16)


def _compress(kvc, cmp_pos_k, cmp_pos_v, w1k, w1v, w2k, w2v, k_norm_cmp):
    b, nch, _ = kvc.shape
    wlo, whi, w2 = _compress_weights(w1k, w1v, w2k, w2v)
    flat = CMP_BLOCK * HEAD_DIM
    out = jax.ShapeDtypeStruct((b, NSA_KV_GROUPS, nch, LANES), BF16)
    out_spec = pl.BlockSpec((1, NSA_KV_GROUPS, nch, LANES), lambda bi: (bi, 0, 0, 0))
    return pl.pallas_call(
        _compress_body,
        grid=(b,),
        in_specs=[pl.BlockSpec((1, nch, CMP_ROW), lambda bi: (bi, 0, 0)),
                  _const_spec((CMP_ROW, CMP_HID_ALL)), _const_spec((CMP_ROW, CMP_HID_ALL)),
                  _const_spec((1, flat)), _const_spec((1, flat)),
                  _const_spec((flat, CMP_HIDDEN)), _const_spec((flat, CMP_HIDDEN)),
                  _const_spec((CMP_HID_ALL, 4 * LANES)), _const_spec((1, LANES))],
        out_specs=[out_spec, out_spec],
        out_shape=[out, out],
        compiler_params=_params(("parallel",)),
        name="compress",
    )(kvc, wlo, whi, cmp_pos_k.reshape(1, flat), cmp_pos_v.reshape(1, flat),
      w1k.astype(BF16), w1v.astype(BF16), w2, _slot_gain(k_norm_cmp))


GROUP_WIDTH = Q_PER_KV * HEAD_DIM


def _store_heads(o_ref, outs):
    o_ref[0] = jnp.concatenate([o[:, :HEAD_DIM] for o in outs], axis=1).astype(BF16)


def _attn_out(b, s, tq):
    spec = pl.BlockSpec((1, tq, GROUP_WIDTH), lambda bi, g, i: (bi, i, g))
    return spec, jax.ShapeDtypeStruct((b, s, NSA_WIDTH), BF16)


def _masked_softmax(s, mask, axis):
    s = jnp.where(mask, s, NEG)
    m = jnp.max(s, axis=axis, keepdims=True)
    e = jnp.exp2(s - m)
    den = jnp.sum(e, axis=axis, keepdims=True)
    return e * jnp.where(m > 0.5 * NEG, 1.0 / den, 0.0)


def _topk_rows(score, k):
    n, t = score.shape
    gone = -3e38

    def peel(_, carry):
        work, cnt, thr = carry
        m = jnp.max(work, axis=0, keepdims=True)
        hit = work == m
        thr = jnp.where(cnt < k, m, thr)
        cnt = cnt + jnp.sum(jnp.where(hit, 1.0, 0.0), axis=0, keepdims=True)
        return jnp.where(hit, gone, work), cnt, thr

    init = (score, jnp.zeros((1, t), F32), jnp.full((1, t), gone, F32))
    _, _, thr = lax.fori_loop(0, k, peel, init, unroll=True)
    above = jnp.where(score > thr, 1.0, 0.0)
    tied = jnp.where(score == thr, 1.0, 0.0)
    room = k - jnp.sum(above, axis=0, keepdims=True)
    lower = lax.broadcasted_iota(jnp.int32, (n, n), 1) < lax.broadcasted_iota(jnp.int32, (n, n), 0)
    tied_before = _dot(jnp.where(lower, 1.0, 0.0).astype(BF16), tied.astype(BF16))
    return jnp.maximum(above, tied * jnp.where(tied_before < room, 1.0, 0.0))


def _cmp_select_body(q_ref, kc_ref, vc_ref, gate_ref, o_ref, qs_ref, *, ns):
    tq = q_ref.shape[2]
    nch = kc_ref.shape[2]
    t0 = pl.program_id(2) * tq
    q = q_ref[0].reshape(Q_PER_KV * tq, LANES)
    s = _dot_nt(q, kc_ref[0, 0]).reshape(Q_PER_KV, tq, nch)
    tok = t0 + lax.broadcasted_iota(jnp.int32, (tq, nch), 0)
    cid = lax.broadcasted_iota(jnp.int32, (tq, nch), 1)
    p = _masked_softmax(s, (cid * CMP_STRIDE + CMP_BLOCK - 1 <= tok)[None], axis=-1)
    o = _dot(p.reshape(Q_PER_KV * tq, nch).astype(BF16), vc_ref[0, 0]).reshape(Q_PER_KV, tq, LANES)
    gate = gate_ref[0, 0]
    _store_heads(o_ref, [o[r] * gate[:, r:r + 1] for r in range(Q_PER_KV)])
    psum = p[0] + p[1] + p[2] + p[3]
    cb = lax.broadcasted_iota(jnp.int32, (nch, LANES), 0)
    jb = lax.broadcasted_iota(jnp.int32, (nch, LANES), 1)
    overlap = jnp.where((cb * CMP_STRIDE < (jb + 1) * SLC_BLOCK) & (cb * CMP_STRIDE + CMP_BLOCK > jb * SLC_BLOCK), 1.0, 0.0)
    imp = _dot(psum, overlap).T[:ns]
    j = lax.broadcasted_iota(jnp.int32, (ns, tq), 0)
    qblk = (t0 + lax.broadcasted_iota(jnp.int32, (ns, tq), 1)) // SLC_BLOCK
    force = (j == 0) | (j == qblk) | (j == qblk - 1)
    score = jnp.where(j <= qblk, imp + FORCE_BONUS * jnp.where(force, 1.0, 0.0), NEG)
    sel_bias = jnp.where(_topk_rows(score, SLC_TOPK) > 0, 0.0, NEG)
    pieces = [jnp.zeros((HEAD_DIM, tq), F32), sel_bias]
    if ns < LANES - HEAD_DIM:
        pieces.append(jnp.zeros((LANES - HEAD_DIM - ns, tq), F32))
    bias = jnp.concatenate(pieces, axis=0).T
    for r in range(Q_PER_KV):
        qs_ref[0, r] = (q_ref[0, r].astype(F32) + bias).astype(BF16)


def _cmp_select(q, kc, vc, gates, tq=256):
    b, _, s, _ = q.shape
    nch = kc.shape[2]
    ns = s // SLC_BLOCK
    assert ns <= LANES - HEAD_DIM
    qspec = pl.BlockSpec((1, Q_PER_KV, tq, LANES), lambda bi, g, i: (bi, g, i, 0))
    cspec = pl.BlockSpec((1, 1, nch, LANES), lambda bi, g, i: (bi, g, 0, 0))
    ospec, oshape = _attn_out(b, s, tq)
    return pl.pallas_call(
        functools.partial(_cmp_select_body, ns=ns),
        grid=(b, NSA_KV_GROUPS, s // tq),
        in_specs=[qspec, cspec, cspec, pl.BlockSpec((1, 1, tq, LANES), lambda bi, g, i: (bi, g, i, 0))],
        out_specs=[ospec, qspec],
        out_shape=[oshape, jax.ShapeDtypeStruct((b, NSA_HEADS, s, LANES), BF16)],
        compiler_params=_params(("parallel", "parallel", "parallel")),
        name="cmp_select",
    )(q, kc, vc, gates)


def _slc_body(q_ref, k_ref, v_ref, gate_ref, o_ref, p_ref, m_ref, acc_ref):
    tq = q_ref.shape[2]
    i = pl.program_id(2)
    n_lane_tiles = tq // LANES

    def scores(r, j):
        k0 = pl.multiple_of(j * tq, tq)
        return _dot_nt(q_ref[0, r], k_ref[0, 0, pl.ds(k0, tq), :]).astype(BF16)

    def pending_values(r, j):
        k0 = pl.multiple_of(j * tq, tq)
        return _dot(p_ref[r], v_ref[0, 0, pl.ds(k0, tq), :])

    def row_max(s):
        part = functools.reduce(jnp.maximum, [s[:, c * LANES:(c + 1) * LANES] for c in range(n_lane_tiles)])
        return jnp.broadcast_to(jnp.max(part, axis=-1, keepdims=True), (tq, LANES))

    def probabilities(s, m):
        return jnp.concatenate([jnp.exp2(s[:, c * LANES:(c + 1) * LANES] - m) for c in range(n_lane_tiles)], axis=1)

    keep = lax.broadcasted_iota(jnp.int32, (tq, tq), 1) <= lax.broadcasted_iota(jnp.int32, (tq, tq), 0)
    for r in range(Q_PER_KV):
        s = jnp.where(keep, scores(r, i), NEG)
        m = row_max(s)
        m_ref[r] = m
        p_ref[r] = probabilities(s, m)
        acc_ref[r] = jnp.zeros((tq, LANES), F32)

    def step(j, carry):
        j_pend = jnp.where(j == 0, i, j - 1)
        for r in range(Q_PER_KV):
            pv = pending_values(r, j_pend)
            s = scores(r, j)
            m = m_ref[r]
            m_new = jnp.maximum(m, row_max(s))
            p_ref[r] = probabilities(s, m_new)
            alpha = jnp.exp2(m.astype(F32) - m_new.astype(F32))
            acc_ref[r] = alpha * (acc_ref[r] + pv)
            m_ref[r] = m_new
        return carry

    lax.fori_loop(0, i, step, 0)
    gate = gate_ref[0, 0]
    outs = []
    for r in range(Q_PER_KV):
        acc = acc_ref[r] + pending_values(r, jnp.where(i == 0, 0, i - 1))
        o = acc / acc[:, HEAD_DIM:HEAD_DIM + 1]
        outs.append(o * gate[:, Q_PER_KV + r:Q_PER_KV + r + 1])
    _store_heads(o_ref, outs)


def _slc_attn(qs, k, v, gates, tq=512):
    b, _, s, _ = qs.shape
    qspec = pl.BlockSpec((1, Q_PER_KV, tq, LANES), lambda bi, g, i: (bi, g, i, 0))
    kspec = pl.BlockSpec((1, 1, s, LANES), lambda bi, g, i: (bi, g, 0, 0))
    ospec, oshape = _attn_out(b, s, tq)
    return pl.pallas_call(
        _slc_body,
        grid=(b, NSA_KV_GROUPS, s // tq),
        in_specs=[qspec, kspec, kspec, pl.BlockSpec((1, 1, tq, LANES), lambda bi, g, i: (bi, g, i, 0))],
        out_specs=ospec,
        out_shape=oshape,
        scratch_shapes=[pltpu.VMEM((Q_PER_KV, tq, tq), BF16), pltpu.VMEM((Q_PER_KV, tq, LANES), BF16),
                        pltpu.VMEM((Q_PER_KV, tq, LANES), F32)],
        compiler_params=_params(("parallel", "parallel", "arbitrary")),
        name="slc_attn",
    )(qs, k, v, gates)


def _win_body(q_ref, k_ref, v_ref, gate_ref, o_ref):
    tq = q_ref.shape[2]
    nprev = WINDOW // tq
    i = pl.program_id(2)
    qpos = i * tq + lax.broadcasted_iota(jnp.int32, (tq, tq), 0)
    col = lax.broadcasted_iota(jnp.int32, (tq, tq), 1)
    keeps, starts = [], []
    for d in range(nprev + 1):
        jt = i - nprev + d
        diff = qpos - (jt * tq + col)
        keeps.append((diff >= 0) & (diff < WINDOW) & (jt >= 0))
        starts.append(pl.multiple_of(jnp.maximum(jt, 0) * tq, tq))
    gate = gate_ref[0, 0]

    outs = []
    for r in range(Q_PER_KV):
        q = q_ref[0, r]
        s = [jnp.where(keep, _dot_nt(q, k_ref[0, 0, pl.ds(k0, tq), :]).astype(BF16), NEG)
             for keep, k0 in zip(keeps, starts)]
        m = jnp.max(functools.reduce(jnp.maximum, s), axis=-1, keepdims=True)
        acc = sum(_dot(jnp.exp2(sd - m), v_ref[0, 0, pl.ds(k0, tq), :]) for sd, k0 in zip(s, starts))
        o = acc / acc[:, HEAD_DIM:HEAD_DIM + 1]
        outs.append(o * gate[:, 2 * Q_PER_KV + r:2 * Q_PER_KV + r + 1])
    _store_heads(o_ref, outs)


def _win_attn(q, k, v, gates, tq=512):
    b, _, s, _ = q.shape
    assert WINDOW % tq == 0
    qspec = pl.BlockSpec((1, Q_PER_KV, tq, LANES), lambda bi, g, i: (bi, g, i, 0))
    kspec = pl.BlockSpec((1, 1, s, LANES), lambda bi, g, i: (bi, g, 0, 0))
    ospec, oshape = _attn_out(b, s, tq)
    return pl.pallas_call(
        _win_body,
        grid=(b, NSA_KV_GROUPS, s // tq),
        in_specs=[qspec, kspec, kspec, pl.BlockSpec((1, 1, tq, LANES), lambda bi, g, i: (bi, g, i, 0))],
        out_specs=ospec,
        out_shape=oshape,
        compiler_params=_params(("parallel", "parallel", "arbitrary")),
        name="win_attn",
    )(q, k, v, gates)


def _cmp_win_body(q_ref, kc_ref, vc_ref, kw_ref, vw_ref, gate_ref, ocmp_ref, qs_ref, owin_ref, *, ns):
    _cmp_select_body(q_ref, kc_ref, vc_ref, gate_ref, ocmp_ref, qs_ref, ns=ns)
    _win_body(q_ref, kw_ref, vw_ref, gate_ref, owin_ref)


def _cmp_win(q, kc, vc, kw, vw, gates, tq=512):
    b, _, s, _ = q.shape
    nch = kc.shape[2]
    ns = s // SLC_BLOCK
    assert ns <= LANES - HEAD_DIM and WINDOW % tq == 0
    qspec = pl.BlockSpec((1, Q_PER_KV, tq, LANES), lambda bi, g, i: (bi, g, i, 0))
    cspec = pl.BlockSpec((1, 1, nch, LANES), lambda bi, g, i: (bi, g, 0, 0))
    kspec = pl.BlockSpec((1, 1, s, LANES), lambda bi, g, i: (bi, g, 0, 0))
    ospec, oshape = _attn_out(b, s, tq)
    return pl.pallas_call(
        functools.partial(_cmp_win_body, ns=ns),
        grid=(b, NSA_KV_GROUPS, s // tq),
        in_specs=[qspec, cspec, cspec, kspec, kspec,
                  pl.BlockSpec((1, 1, tq, LANES), lambda bi, g, i: (bi, g, i, 0))],
        out_specs=[ospec, qspec, ospec],
        out_shape=[oshape, jax.ShapeDtypeStruct((b, NSA_HEADS, s, LANES), BF16), oshape],
        compiler_params=_params(("parallel", "parallel", "arbitrary")),
        name="cmp_win",
    )(q, kc, vc, kw, vw, gates)


def _ssm_disc_body(lr_ref, li_ref, ls_ref, br_ref, bi_ref, ar_ref, ai_ref, bbr_ref, bbi_ref):
    lr, li = lr_ref[...], li_ref[...]
    step = jnp.exp(ls_ref[...])
    mag = jnp.exp(lr * step)
    ar = mag * jnp.cos(li * step)
    ai = mag * jnp.sin(li * step)
    den = lr * lr + li * li
    cr = ((ar - 1.0) * lr + ai * li) / den
    ci = (ai * lr - (ar - 1.0) * li) / den
    br, bi = br_ref[...], bi_ref[...]
    ar_ref[...] = jnp.broadcast_to(ar, ar_ref.shape)
    ai_ref[...] = jnp.broadcast_to(ai, ai_ref.shape)
    bbr_ref[...] = cr * br - ci * bi
    bbi_ref[...] = cr * bi + ci * br


def _ssm_disc(lam_re, lam_im, log_step, b_re, b_im):
    row = lambda a: a.reshape(1, SSM_STATES)
    chan = lambda a: a.transpose(2, 0, 1).reshape(SSM_GROUP, SSM_STATES)
    ls = jnp.broadcast_to(log_step[:, None], (SSM_GROUPS, SSM_STATE))
    return pl.pallas_call(
        _ssm_disc_body,
        out_shape=[jax.ShapeDtypeStruct((SUBLANES, SSM_STATES), F32)] * 2
        + [jax.ShapeDtypeStruct((SSM_GROUP, SSM_STATES), F32)] * 2,
        name="ssm_disc",
    )(row(lam_re), row(lam_im), row(ls), chan(b_re), chan(b_im))


def _group_diag_in(w):
    w = w.reshape(SSM_GROUP, SSM_GROUPS, SSM_STATE)
    eye = jnp.eye(SSM_GROUPS, dtype=w.dtype)
    return (w[None] * eye[:, None, :, None]).reshape(SSM_WIDTH, SSM_STATES)


def _group_diag_out(c):
    eye = jnp.eye(SSM_GROUPS, dtype=c.dtype)
    return (c.transpose(0, 2, 1)[:, :, None, :] * eye[:, None, :, None]).reshape(SSM_STATES, SSM_WIDTH)


SCAN_COLS = 512
SSM_BLOCKS = SSM_STATES // SCAN_COLS
SCAN_CHANS = SSM_WIDTH // SSM_BLOCKS


def _diag_blocks(w):
    r, c = w.shape[0] // SSM_BLOCKS, w.shape[1] // SSM_BLOCKS
    return jnp.stack([w[m * r:(m + 1) * r, m * c:(m + 1) * c] for m in range(SSM_BLOCKS)])


def _ssm_scan_body(u_ref, ar_ref, ai_ref, wbr_ref, wbi_ref, wcr_ref, wci_ref, d_ref, o_ref,
                   xr_ref, xi_ref, vr_ref, vi_ref, row_sc):
    steps = u_ref.shape[0]
    rows = steps * SUBLANES

    @pl.when(pl.program_id(0) == 0)
    def _():
        xr_ref[...] = jnp.zeros_like(xr_ref)
        xi_ref[...] = jnp.zeros_like(xi_ref)

    n_planes = SSM_WIDTH // LANES
    for bi in range(SUBLANES):
        for c in range(n_planes):
            lane0 = bi * SSM_WIDTH + c * LANES
            row_sc[c, pl.ds(bi, steps, stride=SUBLANES), :] = u_ref[:, lane0:lane0 + LANES].astype(F32)
    u = jnp.concatenate([row_sc[c] for c in range(n_planes)], axis=1)
    ub = u.astype(BF16)
    ys = []
    for cb in range(SSM_BLOCKS):
        cs = slice(cb * SCAN_COLS, (cb + 1) * SCAN_COLS)
        ubc = ub[:, cb * SCAN_CHANS:(cb + 1) * SCAN_CHANS]
        vr_ref[:, cs] = _dot(ubc, wbr_ref[cb])
        vi_ref[:, cs] = _dot(ubc, wbi_ref[cb])
        ar, ai = ar_ref[:, cs], ai_ref[:, cs]

        def step(t, carry):
            xr, xi = carry
            r0 = pl.multiple_of(t * SUBLANES, SUBLANES)
            nr = ar * xr - ai * xi + vr_ref[pl.ds(r0, SUBLANES), cs]
            ni = ar * xi + ai * xr + vi_ref[pl.ds(r0, SUBLANES), cs]
            vr_ref[pl.ds(r0, SUBLANES), cs] = nr
            vi_ref[pl.ds(r0, SUBLANES), cs] = ni
            return nr, ni

        xr, xi = lax.fori_loop(0, steps, step, (xr_ref[:, cs], xi_ref[:, cs]), unroll=8)
        xr_ref[:, cs] = xr
        xi_ref[:, cs] = xi
        ys.append(_dot(vr_ref[:, cs].astype(BF16), wcr_ref[cb]) - _dot(vi_ref[:, cs].astype(BF16), wci_ref[cb]))
    y = jnp.concatenate(ys, axis=1) + d_ref[...] * u
    gy = jax.nn.gelu(y)
    for c in range(n_planes):
        row_sc[c] = gy[:, c * LANES:(c + 1) * LANES]
    for bi in range(SUBLANES):
        for c in range(n_planes):
            lane0 = bi * SSM_WIDTH + c * LANES
            o_ref[:, lane0:lane0 + LANES] = row_sc[c, pl.ds(bi, steps, stride=SUBLANES), :].astype(BF16)


def _ssm_scan(u_tm, ar, ai, wbr, wbi, wcr, wci, d, steps=64):
    rows = steps * SUBLANES
    s, width = u_tm.shape
    assert width == SUBLANES * SSM_WIDTH
    return pl.pallas_call(
        _ssm_scan_body,
        grid=(s // steps,),
        in_specs=[pl.BlockSpec((steps, width), lambda i: (i, 0)),
                  _const_spec((SUBLANES, SSM_STATES)), _const_spec((SUBLANES, SSM_STATES)),
                  _const_spec((SSM_BLOCKS, SCAN_CHANS, SCAN_COLS)), _const_spec((SSM_BLOCKS, SCAN_CHANS, SCAN_COLS)),
                  _const_spec((SSM_BLOCKS, SCAN_COLS, SCAN_CHANS)), _const_spec((SSM_BLOCKS, SCAN_COLS, SCAN_CHANS)),
                  _const_spec((1, SSM_WIDTH))],
        out_specs=pl.BlockSpec((steps, width), lambda i: (i, 0)),
        out_shape=jax.ShapeDtypeStruct((s, width), BF16),
        scratch_shapes=[pltpu.VMEM((SUBLANES, SSM_STATES), F32), pltpu.VMEM((SUBLANES, SSM_STATES), F32),
                        pltpu.VMEM((rows, SSM_STATES), F32), pltpu.VMEM((rows, SSM_STATES), F32),
                        pltpu.VMEM((SSM_WIDTH // LANES, rows, LANES), F32)],
        compiler_params=_params(("arbitrary",)),
        name="ssm_scan",
    )(u_tm, ar, ai, wbr, wbi, wcr, wci, d)


def _merge_body(h_ref, g_ref, wgate_ref, oc_ref, os_ref, ow_ref, wn_ref, gy_ref, wglu_ref, wout_ref, o_ref):
    h = h_ref[0]
    hb = _rms(h, g_ref[...]).astype(BF16)
    gates = jax.nn.sigmoid(_dot(hb, wgate_ref[...]))
    o = oc_ref[0].astype(F32) + os_ref[0].astype(F32) + ow_ref[0].astype(F32)
    y_nsa = _dot(o.astype(BF16), wn_ref[...])
    hg = _dot(gy_ref[...], wglu_ref[...])
    y_ssm = hg[:, :D_MODEL] * jax.nn.sigmoid(hg[:, D_MODEL:])
    merged = gates[:, :D_MODEL] * y_nsa + gates[:, D_MODEL:] * y_ssm
    o_ref[0] = h + _dot(merged.astype(BF16), wout_ref[...])


def _merge(h, mix_norm, w_gates, o_cmp, o_slc, o_win, w_nsa_proj, gy, glu_w, w_out, ts=512):
    b, s, _ = h.shape
    row_spec = lambda w: pl.BlockSpec((1, ts, w), lambda bi, i: (bi, i, 0))
    ospec = row_spec(NSA_WIDTH)
    return pl.pallas_call(
        _merge_body,
        grid=(b, s // ts),
        in_specs=[row_spec(D_MODEL), _const_spec((1, D_MODEL)), _const_spec((D_MODEL, 2 * D_MODEL)),
                  ospec, ospec, ospec, _const_spec((NSA_WIDTH, D_MODEL)),
                  pl.BlockSpec((ts, SSM_WIDTH), lambda bi, i: (i, bi)),
                  _const_spec((SSM_WIDTH, 2 * D_MODEL)), _const_spec((D_MODEL, D_MODEL))],
        out_specs=row_spec(D_MODEL),
        out_shape=jax.ShapeDtypeStruct((b, s, D_MODEL), F32),
        compiler_params=_params(("parallel", "parallel")),
        name="merge",
    )(h, mix_norm.reshape(1, D_MODEL), w_gates.astype(BF16), o_cmp, o_slc, o_win, w_nsa_proj.astype(BF16),
      gy, glu_w.astype(BF16), w_out.astype(BF16))


def _layer(h, ffn1_norm, ffn1_w_gate, ffn1_w_up, ffn1_w_down, mix_norm, w_in, q_norm,
           k_norm_cmp, k_norm_slc, k_norm_win, cmp_pos_k, cmp_pos_v, cmp_k_w1, cmp_k_w2,
           cmp_v_w1, cmp_v_w2, w_nsa_proj, ssm_lambda_re, ssm_lambda_im, ssm_log_step,
           ssm_b_re, ssm_b_im, ssm_c_re, ssm_c_im, ssm_d, ssm_glu_w, w_out,
           ffn2_norm, ffn2_w_gate, ffn2_w_up, ffn2_w_down):
    b, s, _ = h.shape
    h = _ffn(h.reshape(b * s, D_MODEL), ffn1_norm, ffn1_w_gate, ffn1_w_up, ffn1_w_down).reshape(b, s, D_MODEL)
    q, kvc, ks, vs, kw, vw, gates, u = _proj(h, mix_norm, w_in[:, :P_END_SRC], q_norm, k_norm_slc, k_norm_win)
    kc, vc = _compress(kvc, cmp_pos_k, cmp_pos_v, cmp_k_w1, cmp_v_w1, cmp_k_w2, cmp_v_w2, k_norm_cmp)
    o_cmp, qs, o_win = _cmp_win(q, kc, vc, kw, vw, gates)
    o_slc = _slc_attn(qs, ks, vs, gates)
    ar, ai, bbr, bbi = _ssm_disc(ssm_lambda_re, ssm_lambda_im, ssm_log_step, ssm_b_re, ssm_b_im)
    assert b == SUBLANES
    gy = _ssm_scan(u, ar, ai,
                   _diag_blocks(_group_diag_in(bbr)).astype(BF16), _diag_blocks(_group_diag_in(bbi)).astype(BF16),
                   _diag_blocks(_group_diag_out(ssm_c_re)).astype(BF16),
                   _diag_blocks(_group_diag_out(ssm_c_im)).astype(BF16),
                   ssm_d.reshape(1, SSM_WIDTH))
    h = _merge(h, mix_norm, w_in[:, P_END_SRC:], o_cmp, o_slc, o_win, w_nsa_proj,
               gy, ssm_glu_w, w_out)
    h = _ffn(h.reshape(b * s, D_MODEL), ffn2_norm, ffn2_w_gate, ffn2_w_up, ffn2_w_down)
    return h.reshape(b, s, D_MODEL)


P_END_SRC = NSA_WIDTH + 6 * KV_WIDTH + 3 * NSA_HEADS + SSM_WIDTH


def kernel(x, ffn1_norm, ffn1_w_gate, ffn1_w_up, ffn1_w_down, mix_norm, w_in, q_norm, k_norm_cmp, k_norm_slc, k_norm_win, cmp_pos_k, cmp_pos_v, cmp_k_w1, cmp_k_w2, cmp_v_w1, cmp_v_w2, w_nsa_proj, ssm_lambda_re, ssm_lambda_im, ssm_log_step, ssm_b_re, ssm_b_im, ssm_c_re, ssm_c_im, ssm_d, ssm_glu_w, w_out, ffn2_norm, ffn2_w_gate, ffn2_w_up, ffn2_w_down):
    params = (ffn1_norm, ffn1_w_gate, ffn1_w_up, ffn1_w_down, mix_norm, w_in, q_norm, k_norm_cmp, k_norm_slc,
              k_norm_win, cmp_pos_k, cmp_pos_v, cmp_k_w1, cmp_k_w2, cmp_v_w1, cmp_v_w2, w_nsa_proj,
              ssm_lambda_re, ssm_lambda_im, ssm_log_step, ssm_b_re, ssm_b_im, ssm_c_re, ssm_c_im, ssm_d,
              ssm_glu_w, w_out, ffn2_norm, ffn2_w_gate, ffn2_w_up, ffn2_w_down)
    h = x.astype(F32)
    for layer in range(ffn1_norm.shape[0]):
        h = _layer(h, *[p[layer] for p in params])
    return h.astype(x.dtype)
```

```python
import functools
import math

import jax
import jax.numpy as jnp
from jax import lax
from jax.experimental import pallas as pl
from jax.experimental.pallas import tpu as pltpu

F32 = jnp.float32
BF16 = jnp.bfloat16

D_MODEL = 1024
NSA_HEADS = 8
NSA_KV_GROUPS = 2
HEAD_DIM = 64
Q_PER_KV = NSA_HEADS // NSA_KV_GROUPS
NSA_WIDTH = NSA_HEADS * HEAD_DIM
KV_WIDTH = NSA_KV_GROUPS * HEAD_DIM
CMP_BLOCK = 32
CMP_STRIDE = 16
CMP_HIDDEN = 256
SLC_BLOCK = 64
SLC_TOPK = 16
WINDOW = 512
FORCE_BONUS = 1000.0
SSM_GROUP = 16
SSM_GROUPS = 32
SSM_STATE = 64
SSM_WIDTH = SSM_GROUPS * SSM_GROUP
SSM_STATES = SSM_GROUPS * SSM_STATE
D_FF = 2816
RMS_EPS = 1e-6
NEG = -1e30

LANES = 128
SUBLANES = 8
VMEM_LIMIT = 56 * 1024 * 1024
MERGE_VMEM_LIMIT = 60 * 1024 * 1024

P_Q = 0
P_KVC = P_Q + NSA_WIDTH
P_KS = P_KVC + 2 * KV_WIDTH
P_VS = P_KS + KV_WIDTH
P_KW = P_VS + KV_WIDTH
P_VW = P_KW + KV_WIDTH
P_GATE = P_VW + KV_WIDTH
P_U = P_GATE + NSA_KV_GROUPS * LANES
P_END = P_U + SSM_WIDTH


def _params(sem, vmem=VMEM_LIMIT):
    return pltpu.CompilerParams(dimension_semantics=sem, vmem_limit_bytes=vmem)


def _const_spec(shape):
    n = len(shape)
    return pl.BlockSpec(shape, lambda *_: (0,) * n, pipeline_mode=pl.Buffered(1))


def _rms(x, g):
    return x * lax.rsqrt(jnp.mean(x * x, axis=-1, keepdims=True) + RMS_EPS) * g


def _slot_rms(x, g):
    ms = jnp.sum(x * x, axis=-1, keepdims=True) * (1.0 / HEAD_DIM)
    return x * lax.rsqrt(ms + RMS_EPS) * g


def _dot(a, b):
    return jnp.dot(a, b, preferred_element_type=F32)


def _dot_nt(a, b):
    return lax.dot_general(a, b, (((1,), (1,)), ((), ())), preferred_element_type=F32)


MXU_DIM = 256
FF_SPLITS = (0, (D_FF // 2 + MXU_DIM - 1) // MXU_DIM * MXU_DIM, D_FF)


def _ffn_body(x_ref, g_ref, wg_ref, wu_ref, wd_ref, o_ref):
    o_ref[...] = _half_swiglu(x_ref[...], g_ref, wg_ref, wu_ref, wd_ref)


def _half_swiglu(x, g_ref, wg_ref, wu_ref, wd_ref):
    hb = _rms(x, g_ref[...]).astype(BF16)
    acc = jnp.zeros_like(x)
    for c in range(len(FF_SPLITS) - 1):
        sl = slice(FF_SPLITS[c], FF_SPLITS[c + 1])
        gate = _dot(hb, wg_ref[:, sl])
        up = _dot(hb, wu_ref[:, sl])
        act = (gate * jax.nn.sigmoid(gate) * up).astype(BF16)
        acc = acc + _dot(act, wd_ref[sl, :])
    return x + 0.5 * acc


def _ffn(x2, g, wg, wu, wd, tm=512):
    t = x2.shape[0]
    return pl.pallas_call(
        _ffn_body,
        grid=(t // tm,),
        in_specs=[pl.BlockSpec((tm, D_MODEL), lambda i: (i, 0)),
                  _const_spec((1, D_MODEL)),
                  _const_spec((D_MODEL, D_FF)),
                  _const_spec((D_MODEL, D_FF)),
                  _const_spec((D_FF, D_MODEL))],
        out_specs=pl.BlockSpec((tm, D_MODEL), lambda i: (i, 0)),
        out_shape=jax.ShapeDtypeStruct((t, D_MODEL), F32),
        compiler_params=_params(("parallel",)),
        name="ffn",
    )(x2, g.reshape(1, D_MODEL), wg.astype(BF16), wu.astype(BF16), wd.astype(BF16))


def _pair_rms(x, gain, low):
    sq = x * x
    ms_low = jnp.sum(jnp.where(low, sq, 0.0), axis=-1, keepdims=True)
    ms_high = jnp.sum(jnp.where(low, 0.0, sq), axis=-1, keepdims=True)
    ms = jnp.where(low, ms_low, ms_high) * (1.0 / HEAD_DIM)
    return x * lax.rsqrt(ms + RMS_EPS) * gain


def _pair_slots(x, low, fill):
    return jnp.where(low, x, fill), jnp.where(low, pltpu.roll(x, HEAD_DIM, 1), fill)


def _proj_body(h_ref, g_ref, w_ref, qn_ref, ksn_ref, kwn_ref,
               q_ref, kvc_ref, ks_ref, vs_ref, kw_ref, vw_ref, gate_ref, u_ref, kv_sc):
    ts = h_ref.shape[1]
    hb = _rms(h_ref[0], g_ref[...]).astype(BF16)
    p = _dot(hb, w_ref[...])
    lane = lax.broadcasted_iota(jnp.int32, (ts, LANES), 1)
    tok = lax.broadcasted_iota(jnp.int32, (ts, LANES), 0) + pl.program_id(1) * ts
    low = lane < HEAD_DIM
    onehot = jnp.where(lane - HEAD_DIM == tok // SLC_BLOCK, 1.0, 0.0)
    qscale = HEAD_DIM ** -0.5 * math.log2(math.e)
    tile = lambda base, c=0: p[:, base + c * LANES:base + (c + 1) * LANES]
    for c in range(NSA_HEADS // 2):
        pair = _pair_slots(_pair_rms(tile(P_Q, c), qn_ref[...], low) * qscale, low, 0.0)
        q_ref[0, 2 * c] = pair[0].astype(BF16)
        q_ref[0, 2 * c + 1] = pair[1].astype(BF16)
    n_planes = 2 * KV_WIDTH // LANES
    for c in range(n_planes):
        kv_sc[c] = tile(P_KVC, c)
    for j in range(CMP_STRIDE):
        for c in range(n_planes):
            lane0 = (j * n_planes + c) * LANES
            kvc_ref[0, :, lane0:lane0 + LANES] = kv_sc[c, pl.ds(j, ts // CMP_STRIDE, stride=CMP_STRIDE), :]
    outs = ((ks_ref, _pair_slots(_pair_rms(tile(P_KS), ksn_ref[...], low), low, onehot)),
            (vs_ref, _pair_slots(tile(P_VS), low, 1.0)),
            (kw_ref, _pair_slots(_pair_rms(tile(P_KW), kwn_ref[...], low), low, 0.0)),
            (vw_ref, _pair_slots(tile(P_VW), low, 1.0)))
    for ref, pair in outs:
        for g in range(NSA_KV_GROUPS):
            ref[0, g] = pair[g].astype(BF16)
    for g in range(NSA_KV_GROUPS):
        gate_ref[0, g] = jax.nn.sigmoid(tile(P_GATE, g))
    u_ref[...] = p[:, P_U:P_END].astype(BF16)


def _slot_gain(g):
    return jnp.pad(g, (0, LANES - HEAD_DIM)).reshape(1, LANES)


def _pair_gain(g):
    return jnp.tile(g, LANES // HEAD_DIM).reshape(1, LANES)


def _proj_weight(w_in):
    n_head_cols = NSA_WIDTH + 6 * KV_WIDTH
    n_gate = 3 * NSA_HEADS
    gate = w_in[:, n_head_cols:n_head_cols + n_gate]
    gate = gate.reshape(D_MODEL, 3, NSA_KV_GROUPS, Q_PER_KV).transpose(0, 2, 1, 3)
    gate = gate.reshape(D_MODEL, NSA_KV_GROUPS, 3 * Q_PER_KV)
    gate = jnp.pad(gate, ((0, 0), (0, 0), (0, LANES - 3 * Q_PER_KV))).reshape(D_MODEL, NSA_KV_GROUPS * LANES)
    cols = [w_in[:, :n_head_cols], gate, w_in[:, n_head_cols + n_gate:]]
    return jnp.concatenate(cols, axis=1).astype(BF16)


def _proj(h, mix_norm, w_in, q_norm, k_norm_slc, k_norm_win, ts=512):
    b, s, _ = h.shape
    hm = lambda n: jax.ShapeDtypeStruct((b, n, s, LANES), BF16)
    hm_spec = lambda n: pl.BlockSpec((1, n, ts, LANES), lambda bi, i: (bi, 0, i, 0))
    row_spec = lambda w: pl.BlockSpec((1, ts, w), lambda bi, i: (bi, i, 0))
    return pl.pallas_call(
        _proj_body,
        grid=(b, s // ts),
        in_specs=[row_spec(D_MODEL), _const_spec((1, D_MODEL)), _const_spec((D_MODEL, P_END)),
                  _const_spec((1, LANES)), _const_spec((1, LANES)), _const_spec((1, LANES))],
        out_specs=[hm_spec(NSA_HEADS),
                   pl.BlockSpec((1, ts // CMP_STRIDE, CMP_ROW), lambda bi, i: (bi, i, 0)),
                   hm_spec(NSA_KV_GROUPS), hm_spec(NSA_KV_GROUPS), hm_spec(NSA_KV_GROUPS), hm_spec(NSA_KV_GROUPS),
                   hm_spec(NSA_KV_GROUPS),
                   pl.BlockSpec((ts, SSM_WIDTH), lambda bi, i: (i, bi))],
        out_shape=[hm(NSA_HEADS), jax.ShapeDtypeStruct((b, s // CMP_STRIDE, CMP_ROW), F32),
                   hm(NSA_KV_GROUPS), hm(NSA_KV_GROUPS), hm(NSA_KV_GROUPS), hm(NSA_KV_GROUPS),
                   jax.ShapeDtypeStruct((b, NSA_KV_GROUPS, s, LANES), F32),
                   jax.ShapeDtypeStruct((s, b * SSM_WIDTH), BF16)],
        scratch_shapes=[pltpu.VMEM((2 * KV_WIDTH // LANES, ts, LANES), F32)],
        compiler_params=_params(("parallel", "parallel")),
        name="proj",
    )(h, mix_norm.reshape(1, D_MODEL), _proj_weight(w_in),
      _pair_gain(q_norm), _pair_gain(k_norm_slc), _pair_gain(k_norm_win))


CMP_ROW = CMP_STRIDE * 2 * KV_WIDTH
CMP_HID_ALL = 4 * CMP_HIDDEN


def _compress_body(c_ref, wlo_ref, whi_ref, pk_ref, pv_ref, w1k_ref, w1v_ref, w2_ref, kn_ref,
                   kc_ref, vc_ref):
    nch = c_ref.shape[1]
    c = c_ref[0].astype(BF16)
    first = _dot(c, wlo_ref[...])
    second = _dot(c, whi_ref[...])
    second = pltpu.roll(second, nch - 1, 0)
    rows8 = lambda r: jnp.broadcast_to(r[...].astype(BF16), (SUBLANES, r.shape[1]))
    bk = _dot(rows8(pk_ref), w1k_ref[...])[0:1]
    bv = _dot(rows8(pv_ref), w1v_ref[...])[0:1]
    bias = jnp.concatenate([bk, bk, bv, bv], axis=1)
    hid = jax.nn.gelu(first + second + bias).astype(BF16)
    out = _dot(hid, w2_ref[...])
    lane = lax.broadcasted_iota(jnp.int32, (nch, LANES), 1)
    low = lane < HEAD_DIM
    for g in range(NSA_KV_GROUPS):
        kc_ref[0, g] = _slot_rms(out[:, g * LANES:(g + 1) * LANES], kn_ref[...]).astype(BF16)
        v = out[:, (NSA_KV_GROUPS + g) * LANES:(NSA_KV_GROUPS + g + 1) * LANES]
        vc_ref[0, g] = jnp.where(low, v, 1.0).astype(BF16)


def _compress_weights(w1k, w1v, w2k, w2v):
    def half(w1, lo):
        w = w1[lo * CMP_STRIDE * HEAD_DIM:(lo + 1) * CMP_STRIDE * HEAD_DIM].reshape(CMP_STRIDE, HEAD_DIM, CMP_HIDDEN)
        return w
    eye4 = jnp.eye(4, dtype=F32)
    def build(lo):
        wk, wv = half(w1k, lo), half(w1v, lo)
        w = jnp.stack([wk, wk, wv, wv], axis=1)
        w = w[:, :, :, None, :] * eye4[None, :, None, :, None]
        return w.reshape(CMP_ROW, CMP_HID_ALL).astype(BF16)
    w2 = jnp.stack([w2k, w2k, w2v, w2v], axis=0)
    w2 = jnp.pad(w2, ((0, 0), (0, 0), (0, LANES - HEAD_DIM)))
    w2 = w2[:, :, None, :] * eye4[:, None, :, None]
    return build(0), build(1), w2.reshape(CMP_HID_ALL, 4 * LANES).astype(BF16)


def _compress(kvc, cmp_pos_k, cmp_pos_v, w1k, w1v, w2k, w2v, k_norm_cmp):
    b, nch, _ = kvc.shape
    wlo, whi, w2 = _compress_weights(w1k, w1v, w2k, w2v)
    flat = CMP_BLOCK * HEAD_DIM
    out = jax.ShapeDtypeStruct((b, NSA_KV_GROUPS, nch, LANES), BF16)
    out_spec = pl.BlockSpec((1, NSA_KV_GROUPS, nch, LANES), lambda bi: (bi, 0, 0, 0))
    return pl.pallas_call(
        _compress_body,
        grid=(b,),
        in_specs=[pl.BlockSpec((1, nch, CMP_ROW), lambda bi: (bi, 0, 0)),
                  _const_spec((CMP_ROW, CMP_HID_ALL)), _const_spec((CMP_ROW, CMP_HID_ALL)),
                  _const_spec((1, flat)), _const_spec((1, flat)),
                  _const_spec((flat, CMP_HIDDEN)), _const_spec((flat, CMP_HIDDEN)),
                  _const_spec((CMP_HID_ALL, 4 * LANES)), _const_spec((1, LANES))],
        out_specs=[out_spec, out_spec],
        out_shape=[out, out],
        compiler_params=_params(("parallel",)),
        name="compress",
    )(kvc, wlo, whi, cmp_pos_k.reshape(1, flat), cmp_pos_v.reshape(1, flat),
      w1k.astype(BF16), w1v.astype(BF16), w2, _slot_gain(k_norm_cmp))


GROUP_WIDTH = Q_PER_KV * HEAD_DIM


def _store_heads(o_ref, outs):
    o_ref[0] = jnp.concatenate([o[:, :HEAD_DIM] for o in outs], axis=1).astype(BF16)


def _attn_out(b, s, tq):
    spec = pl.BlockSpec((1, tq, GROUP_WIDTH), lambda bi, g, i: (bi, i, g))
    return spec, jax.ShapeDtypeStruct((b, s, NSA_WIDTH), BF16)


def _masked_softmax(s, mask, axis):
    s = jnp.where(mask, s, NEG)
    m = jnp.max(s, axis=axis, keepdims=True)
    e = jnp.exp2(s - m)
    den = jnp.sum(e, axis=axis, keepdims=True)
    return e * jnp.where(m > 0.5 * NEG, 1.0 / den, 0.0)


def _topk_rows(score, k):
    n, t = score.shape
    gone = -3e38

    def peel(_, carry):
        work, cnt, thr = carry
        m = jnp.max(work, axis=0, keepdims=True)
        hit = work == m
        thr = jnp.where(cnt < k, m, thr)
        cnt = cnt + jnp.sum(jnp.where(hit, 1.0, 0.0), axis=0, keepdims=True)
        return jnp.where(hit, gone, work), cnt, thr

    init = (score, jnp.zeros((1, t), F32), jnp.full((1, t), gone, F32))
    _, _, thr = lax.fori_loop(0, k, peel, init, unroll=True)
    above = jnp.where(score > thr, 1.0, 0.0)
    tied = jnp.where(score == thr, 1.0, 0.0)
    room = k - jnp.sum(above, axis=0, keepdims=True)
    lower = lax.broadcasted_iota(jnp.int32, (n, n), 1) < lax.broadcasted_iota(jnp.int32, (n, n), 0)
    tied_before = _dot(jnp.where(lower, 1.0, 0.0).astype(BF16), tied.astype(BF16))
    return jnp.maximum(above, tied * jnp.where(tied_before < room, 1.0, 0.0))


def _cmp_select_body(q_ref, kc_ref, vc_ref, gate_ref, o_ref, qs_ref, *, ns):
    tq = q_ref.shape[2]
    nch = kc_ref.shape[2]
    t0 = pl.program_id(2) * tq
    q = q_ref[0].reshape(Q_PER_KV * tq, LANES)
    s = _dot_nt(q, kc_ref[0, 0]).reshape(Q_PER_KV, tq, nch)
    tok = t0 + lax.broadcasted_iota(jnp.int32, (tq, nch), 0)
    cid = lax.broadcasted_iota(jnp.int32, (tq, nch), 1)
    p = _masked_softmax(s, (cid * CMP_STRIDE + CMP_BLOCK - 1 <= tok)[None], axis=-1)
    o = _dot(p.reshape(Q_PER_KV * tq, nch).astype(BF16), vc_ref[0, 0]).reshape(Q_PER_KV, tq, LANES)
    gate = gate_ref[0, 0]
    _store_heads(o_ref, [o[r] * gate[:, r:r + 1] for r in range(Q_PER_KV)])
    psum = p[0] + p[1] + p[2] + p[3]
    cb = lax.broadcasted_iota(jnp.int32, (nch, LANES), 0)
    jb = lax.broadcasted_iota(jnp.int32, (nch, LANES), 1)
    overlap = jnp.where((cb * CMP_STRIDE < (jb + 1) * SLC_BLOCK) & (cb * CMP_STRIDE + CMP_BLOCK > jb * SLC_BLOCK), 1.0, 0.0)
    imp = _dot(psum, overlap).T[:ns]
    j = lax.broadcasted_iota(jnp.int32, (ns, tq), 0)
    qblk = (t0 + lax.broadcasted_iota(jnp.int32, (ns, tq), 1)) // SLC_BLOCK
    force = (j == 0) | (j == qblk) | (j == qblk - 1)
    score = jnp.where(j <= qblk, imp + FORCE_BONUS * jnp.where(force, 1.0, 0.0), NEG)
    sel_bias = jnp.where(_topk_rows(score, SLC_TOPK) > 0, 0.0, NEG)
    pieces = [jnp.zeros((HEAD_DIM, tq), F32), sel_bias]
    if ns < LANES - HEAD_DIM:
        pieces.append(jnp.zeros((LANES - HEAD_DIM - ns, tq), F32))
    bias = jnp.concatenate(pieces, axis=0).T
    for r in range(Q_PER_KV):
        qs_ref[0, r] = (q_ref[0, r].astype(F32) + bias).astype(BF16)


def _cmp_select(q, kc, vc, gates, tq=256):
    b, _, s, _ = q.shape
    nch = kc.shape[2]
    ns = s // SLC_BLOCK
    assert ns <= LANES - HEAD_DIM
    qspec = pl.BlockSpec((1, Q_PER_KV, tq, LANES), lambda bi, g, i: (bi, g, i, 0))
    cspec = pl.BlockSpec((1, 1, nch, LANES), lambda bi, g, i: (bi, g, 0, 0))
    ospec, oshape = _attn_out(b, s, tq)
    return pl.pallas_call(
        functools.partial(_cmp_select_body, ns=ns),
        grid=(b, NSA_KV_GROUPS, s // tq),
        in_specs=[qspec, cspec, cspec, pl.BlockSpec((1, 1, tq, LANES), lambda bi, g, i: (bi, g, i, 0))],
        out_specs=[ospec, qspec],
        out_shape=[oshape, jax.ShapeDtypeStruct((b, NSA_HEADS, s, LANES), BF16)],
        compiler_params=_params(("parallel", "parallel", "parallel")),
        name="cmp_select",
    )(q, kc, vc, gates)


def _slc_body(q_ref, k_ref, v_ref, gate_ref, o_ref, p_ref, m_ref, acc_ref):
    tq = q_ref.shape[2]
    i = pl.program_id(2)
    n_lane_tiles = tq // LANES

    def scores(r, j):
        k0 = pl.multiple_of(j * tq, tq)
        return _dot_nt(q_ref[0, r], k_ref[0, 0, pl.ds(k0, tq), :]).astype(BF16)

    def pending_values(r, j):
        k0 = pl.multiple_of(j * tq, tq)
        return _dot(p_ref[r], v_ref[0, 0, pl.ds(k0, tq), :])

    def row_max(s):
        part = functools.reduce(jnp.maximum, [s[:, c * LANES:(c + 1) * LANES] for c in range(n_lane_tiles)])
        return jnp.broadcast_to(jnp.max(part, axis=-1, keepdims=True), (tq, LANES))

    def probabilities(s, m):
        return jnp.concatenate([jnp.exp2(s[:, c * LANES:(c + 1) * LANES] - m) for c in range(n_lane_tiles)], axis=1)

    keep = lax.broadcasted_iota(jnp.int32, (tq, tq), 1) <= lax.broadcasted_iota(jnp.int32, (tq, tq), 0)
    for r in range(Q_PER_KV):
        s = jnp.where(keep, scores(r, i), NEG)
        m = row_max(s)
        m_ref[r] = m
        p_ref[r] = probabilities(s, m)
        acc_ref[r] = jnp.zeros((tq, LANES), F32)

    def step(j, carry):
        j_pend = jnp.where(j == 0, i, j - 1)
        for r in range(Q_PER_KV):
            pv = pending_values(r, j_pend)
            s = scores(r, j)
            m = m_ref[r]
            m_new = jnp.maximum(m, row_max(s))
            p_ref[r] = probabilities(s, m_new)
            alpha = jnp.exp2(m.astype(F32) - m_new.astype(F32))
            acc_ref[r] = alpha * (acc_ref[r] + pv)
            m_ref[r] = m_new
        return carry

    lax.fori_loop(0, i, step, 0)
    gate = gate_ref[0, 0]
    outs = []
    for r in range(Q_PER_KV):
        acc = acc_ref[r] + pending_values(r, jnp.where(i == 0, 0, i - 1))
        o = acc / acc[:, HEAD_DIM:HEAD_DIM + 1]
        outs.append(o * gate[:, Q_PER_KV + r:Q_PER_KV + r + 1])
    _store_heads(o_ref, outs)


def _slc_attn(qs, k, v, gates, tq=512):
    b, _, s, _ = qs.shape
    qspec = pl.BlockSpec((1, Q_PER_KV, tq, LANES), lambda bi, g, i: (bi, g, i, 0))
    kspec = pl.BlockSpec((1, 1, s, LANES), lambda bi, g, i: (bi, g, 0, 0))
    ospec, oshape = _attn_out(b, s, tq)
    return pl.pallas_call(
        _slc_body,
        grid=(b, NSA_KV_GROUPS, s // tq),
        in_specs=[qspec, kspec, kspec, pl.BlockSpec((1, 1, tq, LANES), lambda bi, g, i: (bi, g, i, 0))],
        out_specs=ospec,
        out_shape=oshape,
        scratch_shapes=[pltpu.VMEM((Q_PER_KV, tq, tq), BF16), pltpu.VMEM((Q_PER_KV, tq, LANES), BF16),
                        pltpu.VMEM((Q_PER_KV, tq, LANES), F32)],
        compiler_params=_params(("parallel", "parallel", "arbitrary")),
        name="slc_attn",
    )(qs, k, v, gates)


def _win_body(q_ref, k_ref, v_ref, gate_ref, o_ref):
    tq = q_ref.shape[2]
    nprev = WINDOW // tq
    i = pl.program_id(2)
    qpos = i * tq + lax.broadcasted_iota(jnp.int32, (tq, tq), 0)
    col = lax.broadcasted_iota(jnp.int32, (tq, tq), 1)
    keeps, starts = [], []
    for d in range(nprev + 1):
        jt = i - nprev + d
        diff = qpos - (jt * tq + col)
        keeps.append((diff >= 0) & (diff < WINDOW) & (jt >= 0))
        starts.append(pl.multiple_of(jnp.maximum(jt, 0) * tq, tq))
    gate = gate_ref[0, 0]

    outs = []
    for r in range(Q_PER_KV):
        q = q_ref[0, r]
        s = [jnp.where(keep, _dot_nt(q, k_ref[0, 0, pl.ds(k0, tq), :]).astype(BF16), NEG)
             for keep, k0 in zip(keeps, starts)]
        m = jnp.max(functools.reduce(jnp.maximum, s), axis=-1, keepdims=True)
        acc = sum(_dot(jnp.exp2(sd - m), v_ref[0, 0, pl.ds(k0, tq), :]) for sd, k0 in zip(s, starts))
        o = acc / acc[:, HEAD_DIM:HEAD_DIM + 1]
        outs.append(o * gate[:, 2 * Q_PER_KV + r:2 * Q_PER_KV + r + 1])
    _store_heads(o_ref, outs)


def _win_attn(q, k, v, gates, tq=512):
    b, _, s, _ = q.shape
    assert WINDOW % tq == 0
    qspec = pl.BlockSpec((1, Q_PER_KV, tq, LANES), lambda bi, g, i: (bi, g, i, 0))
    kspec = pl.BlockSpec((1, 1, s, LANES), lambda bi, g, i: (bi, g, 0, 0))
    ospec, oshape = _attn_out(b, s, tq)
    return pl.pallas_call(
        _win_body,
        grid=(b, NSA_KV_GROUPS, s // tq),
        in_specs=[qspec, kspec, kspec, pl.BlockSpec((1, 1, tq, LANES), lambda bi, g, i: (bi, g, i, 0))],
        out_specs=ospec,
        out_shape=oshape,
        compiler_params=_params(("parallel", "parallel", "arbitrary")),
        name="win_attn",
    )(q, k, v, gates)


def _cmp_win_body(q_ref, kc_ref, vc_ref, kw_ref, vw_ref, gate_ref, ocmp_ref, qs_ref, owin_ref, *, ns):
    _cmp_select_body(q_ref, kc_ref, vc_ref, gate_ref, ocmp_ref, qs_ref, ns=ns)
    _win_body(q_ref, kw_ref, vw_ref, gate_ref, owin_ref)


def _cmp_win(q, kc, vc, kw, vw, gates, tq=512):
    b, _, s, _ = q.shape
    nch = kc.shape[2]
    ns = s // SLC_BLOCK
    assert ns <= LANES - HEAD_DIM and WINDOW % tq == 0
    qspec = pl.BlockSpec((1, Q_PER_KV, tq, LANES), lambda bi, g, i: (bi, g, i, 0))
    cspec = pl.BlockSpec((1, 1, nch, LANES), lambda bi, g, i: (bi, g, 0, 0))
    kspec = pl.BlockSpec((1, 1, s, LANES), lambda bi, g, i: (bi, g, 0, 0))
    ospec, oshape = _attn_out(b, s, tq)
    return pl.pallas_call(
        functools.partial(_cmp_win_body, ns=ns),
        grid=(b, NSA_KV_GROUPS, s // tq),
        in_specs=[qspec, cspec, cspec, kspec, kspec,
                  pl.BlockSpec((1, 1, tq, LANES), lambda bi, g, i: (bi, g, i, 0))],
        out_specs=[ospec, qspec, ospec],
        out_shape=[oshape, jax.ShapeDtypeStruct((b, NSA_HEADS, s, LANES), BF16), oshape],
        compiler_params=_params(("parallel", "parallel", "arbitrary")),
        name="cmp_win",
    )(q, kc, vc, kw, vw, gates)


def _ssm_disc_body(lr_ref, li_ref, ls_ref, br_ref, bi_ref, ar_ref, ai_ref, bbr_ref, bbi_ref):
    lr, li = lr_ref[...], li_ref[...]
    step = jnp.exp(ls_ref[...])
    mag = jnp.exp(lr * step)
    ar = mag * jnp.cos(li * step)
    ai = mag * jnp.sin(li * step)
    den = lr * lr + li * li
    cr = ((ar - 1.0) * lr + ai * li) / den
    ci = (ai * lr - (ar - 1.0) * li) / den
    br, bi = br_ref[...], bi_ref[...]
    ar_ref[...] = jnp.broadcast_to(ar, ar_ref.shape)
    ai_ref[...] = jnp.broadcast_to(ai, ai_ref.shape)
    bbr_ref[...] = cr * br - ci * bi
    bbi_ref[...] = cr * bi + ci * br


def _ssm_disc(lam_re, lam_im, log_step, b_re, b_im):
    row = lambda a: a.reshape(1, SSM_STATES)
    chan = lambda a: a.transpose(2, 0, 1).reshape(SSM_GROUP, SSM_STATES)
    ls = jnp.broadcast_to(log_step[:, None], (SSM_GROUPS, SSM_STATE))
    return pl.pallas_call(
        _ssm_disc_body,
        out_shape=[jax.ShapeDtypeStruct((SUBLANES, SSM_STATES), F32)] * 2
        + [jax.ShapeDtypeStruct((SSM_GROUP, SSM_STATES), F32)] * 2,
        name="ssm_disc",
    )(row(lam_re), row(lam_im), row(ls), chan(b_re), chan(b_im))


def _group_diag_in(w):
    w = w.reshape(SSM_GROUP, SSM_GROUPS, SSM_STATE)
    eye = jnp.eye(SSM_GROUPS, dtype=w.dtype)
    return (w[None] * eye[:, None, :, None]).reshape(SSM_WIDTH, SSM_STATES)


def _group_diag_out(c):
    eye = jnp.eye(SSM_GROUPS, dtype=c.dtype)
    return (c.transpose(0, 2, 1)[:, :, None, :] * eye[:, None, :, None]).reshape(SSM_STATES, SSM_WIDTH)


SCAN_COLS = 512
SSM_BLOCKS = SSM_STATES // SCAN_COLS
SCAN_CHANS = SSM_WIDTH // SSM_BLOCKS


def _diag_blocks(w):
    r, c = w.shape[0] // SSM_BLOCKS, w.shape[1] // SSM_BLOCKS
    return jnp.stack([w[m * r:(m + 1) * r, m * c:(m + 1) * c] for m in range(SSM_BLOCKS)])


def _ssm_scan_body(u_ref, ar_ref, ai_ref, wbr_ref, wbi_ref, wcr_ref, wci_ref, d_ref, o_ref,
                   xr_ref, xi_ref, vr_ref, vi_ref, row_sc):
    steps = u_ref.shape[0]
    rows = steps * SUBLANES

    @pl.when(pl.program_id(0) == 0)
    def _():
        xr_ref[...] = jnp.zeros_like(xr_ref)
        xi_ref[...] = jnp.zeros_like(xi_ref)

    n_planes = SSM_WIDTH // LANES
    for bi in range(SUBLANES):
        for c in range(n_planes):
            lane0 = bi * SSM_WIDTH + c * LANES
            row_sc[c, pl.ds(bi, steps, stride=SUBLANES), :] = u_ref[:, lane0:lane0 + LANES].astype(F32)
    u = jnp.concatenate([row_sc[c] for c in range(n_planes)], axis=1)
    ub = u.astype(BF16)
    ys = []
    for cb in range(SSM_BLOCKS):
        cs = slice(cb * SCAN_COLS, (cb + 1) * SCAN_COLS)
        ubc = ub[:, cb * SCAN_CHANS:(cb + 1) * SCAN_CHANS]
        vr_ref[:, cs] = _dot(ubc, wbr_ref[cb])
        vi_ref[:, cs] = _dot(ubc, wbi_ref[cb])
        ar, ai = ar_ref[:, cs], ai_ref[:, cs]

        def step(t, carry):
            xr, xi = carry
            r0 = pl.multiple_of(t * SUBLANES, SUBLANES)
            nr = ar * xr - ai * xi + vr_ref[pl.ds(r0, SUBLANES), cs]
            ni = ar * xi + ai * xr + vi_ref[pl.ds(r0, SUBLANES), cs]
            vr_ref[pl.ds(r0, SUBLANES), cs] = nr
            vi_ref[pl.ds(r0, SUBLANES), cs] = ni
            return nr, ni

        xr, xi = lax.fori_loop(0, steps, step, (xr_ref[:, cs], xi_ref[:, cs]), unroll=8)
        xr_ref[:, cs] = xr
        xi_ref[:, cs] = xi
        ys.append(_dot(vr_ref[:, cs].astype(BF16), wcr_ref[cb]) - _dot(vi_ref[:, cs].astype(BF16), wci_ref[cb]))
    y = jnp.concatenate(ys, axis=1) + d_ref[...] * u
    gy = jax.nn.gelu(y)
    for c in range(n_planes):
        row_sc[c] = gy[:, c * LANES:(c + 1) * LANES]
    for bi in range(SUBLANES):
        for c in range(n_planes):
            lane0 = bi * SSM_WIDTH + c * LANES
            o_ref[:, lane0:lane0 + LANES] = row_sc[c, pl.ds(bi, steps, stride=SUBLANES), :].astype(BF16)


def _ssm_scan(u_tm, ar, ai, wbr, wbi, wcr, wci, d, steps=128):
    rows = steps * SUBLANES
    s, width = u_tm.shape
    assert width == SUBLANES * SSM_WIDTH
    return pl.pallas_call(
        _ssm_scan_body,
        grid=(s // steps,),
        in_specs=[pl.BlockSpec((steps, width), lambda i: (i, 0)),
                  _const_spec((SUBLANES, SSM_STATES)), _const_spec((SUBLANES, SSM_STATES)),
                  _const_spec((SSM_BLOCKS, SCAN_CHANS, SCAN_COLS)), _const_spec((SSM_BLOCKS, SCAN_CHANS, SCAN_COLS)),
                  _const_spec((SSM_BLOCKS, SCAN_COLS, SCAN_CHANS)), _const_spec((SSM_BLOCKS, SCAN_COLS, SCAN_CHANS)),
                  _const_spec((1, SSM_WIDTH))],
        out_specs=pl.BlockSpec((steps, width), lambda i: (i, 0)),
        out_shape=jax.ShapeDtypeStruct((s, width), BF16),
        scratch_shapes=[pltpu.VMEM((SUBLANES, SSM_STATES), F32), pltpu.VMEM((SUBLANES, SSM_STATES), F32),
                        pltpu.VMEM((rows, SSM_STATES), F32), pltpu.VMEM((rows, SSM_STATES), F32),
                        pltpu.VMEM((SSM_WIDTH // LANES, rows, LANES), F32)],
        compiler_params=_params(("arbitrary",)),
        name="ssm_scan",
    )(u_tm, ar, ai, wbr, wbi, wcr, wci, d)


def _merge_body(h_ref, g_ref, wgate_ref, oc_ref, os_ref, ow_ref, wn_ref, gy_ref, wglu_ref, wout_ref,
                g2_ref, wg_ref, wu_ref, wd_ref, o_ref):
    h = h_ref[0]
    hb = _rms(h, g_ref[...]).astype(BF16)
    gates = jax.nn.sigmoid(_dot(hb, wgate_ref[...]))
    o = oc_ref[0].astype(F32) + os_ref[0].astype(F32) + ow_ref[0].astype(F32)
    y_nsa = _dot(o.astype(BF16), wn_ref[...])
    hg = _dot(gy_ref[...], wglu_ref[...])
    y_ssm = hg[:, :D_MODEL] * jax.nn.sigmoid(hg[:, D_MODEL:])
    merged = gates[:, :D_MODEL] * y_nsa + gates[:, D_MODEL:] * y_ssm
    h = h + _dot(merged.astype(BF16), wout_ref[...])
    o_ref[0] = _half_swiglu(h, g2_ref, wg_ref, wu_ref, wd_ref)


def _merge(h, mix_norm, w_gates, o_cmp, o_slc, o_win, w_nsa_proj, gy, glu_w, w_out,
           ffn_norm, ffn_w_gate, ffn_w_up, ffn_w_down, ts=512):
    b, s, _ = h.shape
    row_spec = lambda w: pl.BlockSpec((1, ts, w), lambda bi, i: (bi, i, 0))
    ospec = row_spec(NSA_WIDTH)
    return pl.pallas_call(
        _merge_body,
        grid=(b, s // ts),
        in_specs=[row_spec(D_MODEL), _const_spec((1, D_MODEL)), _const_spec((D_MODEL, 2 * D_MODEL)),
                  ospec, ospec, ospec, _const_spec((NSA_WIDTH, D_MODEL)),
                  pl.BlockSpec((ts, SSM_WIDTH), lambda bi, i: (i, bi)),
                  _const_spec((SSM_WIDTH, 2 * D_MODEL)), _const_spec((D_MODEL, D_MODEL)),
                  _const_spec((1, D_MODEL)), _const_spec((D_MODEL, D_FF)), _const_spec((D_MODEL, D_FF)),
                  _const_spec((D_FF, D_MODEL))],
        out_specs=row_spec(D_MODEL),
        out_shape=jax.ShapeDtypeStruct((b, s, D_MODEL), F32),
        compiler_params=_params(("parallel", "parallel"), MERGE_VMEM_LIMIT),
        name="merge_ffn",
    )(h, mix_norm.reshape(1, D_MODEL), w_gates.astype(BF16), o_cmp, o_slc, o_win, w_nsa_proj.astype(BF16),
      gy, glu_w.astype(BF16), w_out.astype(BF16),
      ffn_norm.reshape(1, D_MODEL), ffn_w_gate.astype(BF16), ffn_w_up.astype(BF16), ffn_w_down.astype(BF16))


def _layer(h, ffn1_norm, ffn1_w_gate, ffn1_w_up, ffn1_w_down, mix_norm, w_in, q_norm,
           k_norm_cmp, k_norm_slc, k_norm_win, cmp_pos_k, cmp_pos_v, cmp_k_w1, cmp_k_w2,
           cmp_v_w1, cmp_v_w2, w_nsa_proj, ssm_lambda_re, ssm_lambda_im, ssm_log_step,
           ssm_b_re, ssm_b_im, ssm_c_re, ssm_c_im, ssm_d, ssm_glu_w, w_out,
           ffn2_norm, ffn2_w_gate, ffn2_w_up, ffn2_w_down):
    b, s, _ = h.shape
    h = _ffn(h.reshape(b * s, D_MODEL), ffn1_norm, ffn1_w_gate, ffn1_w_up, ffn1_w_down).reshape(b, s, D_MODEL)
    q, kvc, ks, vs, kw, vw, gates, u = _proj(h, mix_norm, w_in[:, :P_END_SRC], q_norm, k_norm_slc, k_norm_win)
    kc, vc = _compress(kvc, cmp_pos_k, cmp_pos_v, cmp_k_w1, cmp_v_w1, cmp_k_w2, cmp_v_w2, k_norm_cmp)
    o_cmp, qs, o_win = _cmp_win(q, kc, vc, kw, vw, gates)
    o_slc = _slc_attn(qs, ks, vs, gates)
    ar, ai, bbr, bbi = _ssm_disc(ssm_lambda_re, ssm_lambda_im, ssm_log_step, ssm_b_re, ssm_b_im)
    assert b == SUBLANES
    gy = _ssm_scan(u, ar, ai,
                   _diag_blocks(_group_diag_in(bbr)).astype(BF16), _diag_blocks(_group_diag_in(bbi)).astype(BF16),
                   _diag_blocks(_group_diag_out(ssm_c_re)).astype(BF16),
                   _diag_blocks(_group_diag_out(ssm_c_im)).astype(BF16),
                   ssm_d.reshape(1, SSM_WIDTH))
    h = _merge(h, mix_norm, w_in[:, P_END_SRC:], o_cmp, o_slc, o_win, w_nsa_proj,
               gy, ssm_glu_w, w_out, ffn2_norm, ffn2_w_gate, ffn2_w_up, ffn2_w_down)
    return h


P_END_SRC = NSA_WIDTH + 6 * KV_WIDTH + 3 * NSA_HEADS + SSM_WIDTH


def kernel(x, ffn1_norm, ffn1_w_gate, ffn1_w_up, ffn1_w_down, mix_norm, w_in, q_norm, k_norm_cmp, k_norm_slc, k_norm_win, cmp_pos_k, cmp_pos_v, cmp_k_w1, cmp_k_w2, cmp_v_w1, cmp_v_w2, w_nsa_proj, ssm_lambda_re, ssm_lambda_im, ssm_log_step, ssm_b_re, ssm_b_im, ssm_c_re, ssm_c_im, ssm_d, ssm_glu_w, w_out, ffn2_norm, ffn2_w_gate, ffn2_w_up, ffn2_w_down):
    params = (ffn1_norm, ffn1_w_gate, ffn1_w_up, ffn1_w_down, mix_norm, w_in, q_norm, k_norm_cmp, k_norm_slc,
              k_norm_win, cmp_pos_k, cmp_pos_v, cmp_k_w1, cmp_k_w2, cmp_v_w1, cmp_v_w2, w_nsa_proj,
              ssm_lambda_re, ssm_lambda_im, ssm_log_step, ssm_b_re, ssm_b_im, ssm_c_re, ssm_c_im, ssm_d,
              ssm_glu_w, w_out, ffn2_norm, ffn2_w_gate, ffn2_w_up, ffn2_w_down)
    h = x.astype(F32)
    for layer in range(ffn1_norm.shape[0]):
        h = _layer(h, *[p[layer] for p in params])
    return h.astype(x.dtype)
```
